```python
import math
import jax, jax.numpy as jnp
from jax import lax
import numpy as np

D_MODEL = 2048
BATCH = 2
SEQ = 16384
DEPTH = 2

CHUNK = 64
N_A_LAYERS = DEPTH // 2
N_B_LAYERS = DEPTH - N_A_LAYERS
SSM_GROUP = 16
SSM_GROUPS = D_MODEL // SSM_GROUP
SSM_STATE = 64
SCAN_BLOCK = 128
DT_MIN = 1e-3
DT_MAX = 1e-1
N_HEADS = 16
Q_LORA_RANK = 512
KV_LORA_RANK = 512
QK_NOPE = 128
QK_ROPE = 64
V_HEAD = 128
ROPE_THETA = 10000.0
ATTN_SCALE = (QK_NOPE + QK_ROPE) ** -0.5
QBLK = 128
D_FF = 5632
CONV_W = 3
EPS = 1e-6

kernel_name = "s5_mla_yoco_sandwich_convffn"


def rmsnorm(x, g):
    x32 = x.astype(jnp.float32)
    y = x32 * lax.rsqrt(jnp.mean(x32 * x32, axis=-1, keepdims=True) + EPS) * g.astype(jnp.float32)
    return y.astype(x.dtype)


def rope_cos_sin(positions):
    freqs = ROPE_THETA ** (-jnp.arange(0, QK_ROPE, 2, dtype=jnp.float32) / QK_ROPE)
    ang = positions.astype(jnp.float32)[..., None] * freqs
    return jnp.cos(ang), jnp.sin(ang)


def apply_rope(x, cos, sin):
    x32 = x.astype(jnp.float32)
    x1, x2 = jnp.split(x32, 2, axis=-1)
    out = jnp.concatenate([x1 * cos - x2 * sin, x2 * cos + x1 * sin], axis=-1)
    return out.astype(x.dtype)


def s5_mixer(u, a_re, a_im, log_dt, b_re, b_im, c_re, c_im, d_skip, w_glu, b_glu):
    dtype = u.dtype
    Bsz, S, _ = u.shape
    f32 = jnp.float32
    a_re, a_im = a_re.astype(f32), a_im.astype(f32)
    b_re, b_im = b_re.astype(f32), b_im.astype(f32)
    c_re, c_im = c_re.astype(f32), c_im.astype(f32)
    d_skip = d_skip.astype(f32)
    dt = jnp.exp(log_dt.astype(f32))[:, None]
    mag = jnp.exp(dt * a_re)
    ab_re = mag * jnp.cos(dt * a_im)
    ab_im = mag * jnp.sin(dt * a_im)
    den = a_re * a_re + a_im * a_im
    nr, ni = ab_re - 1.0, ab_im
    coef_re = ((nr * a_re + ni * a_im) / den)[..., None]
    coef_im = ((ni * a_re - nr * a_im) / den)[..., None]
    bb_re = coef_re * b_re - coef_im * b_im
    bb_im = coef_re * b_im + coef_im * b_re

    nblk = S // SCAN_BLOCK
    u32 = u.astype(f32).reshape(Bsz, nblk, SCAN_BLOCK, SSM_GROUPS, SSM_GROUP).transpose(1, 0, 2, 3, 4)

    def combine(e1, e2):
        ar1, ai1, br1, bi1 = e1
        ar2, ai2, br2, bi2 = e2
        return (ar2 * ar1 - ai2 * ai1, ar2 * ai1 + ai2 * ar1,
                ar2 * br1 - ai2 * bi1 + br2, ar2 * bi1 + ai2 * br1 + bi2)

    def step(carry, ub):
        h_re, h_im = carry
        x_re = jnp.einsum('blgp,gnp->blgn', ub, bb_re)
        x_im = jnp.einsum('blgp,gnp->blgn', ub, bb_im)
        a_r = jnp.broadcast_to(ab_re, x_re.shape)
        a_i = jnp.broadcast_to(ab_im, x_re.shape)
        acc_re, acc_im, hl_re, hl_im = lax.associative_scan(combine, (a_r, a_i, x_re, x_im), axis=1)
        hs_re = acc_re * h_re[:, None] - acc_im * h_im[:, None] + hl_re
        hs_im = acc_re * h_im[:, None] + acc_im * h_re[:, None] + hl_im
        y = (jnp.einsum('blgn,gpn->blgp', hs_re, c_re) - jnp.einsum('blgn,gpn->blgp', hs_im, c_im)
             + d_skip * ub)
        return (hs_re[:, -1], hs_im[:, -1]), y

    h0 = jnp.zeros((Bsz, SSM_GROUPS, SSM_STATE), f32)
    _, ys = lax.scan(step, (h0, h0), u32)
    y = ys.transpose(1, 0, 2, 3, 4).reshape(Bsz, S, D_MODEL)
    y = jax.nn.gelu(y)
    val, gate = jnp.split(y @ w_glu.astype(f32) + b_glu.astype(f32), 2, axis=-1)
    return (val * jax.nn.sigmoid(gate)).astype(dtype)


def shared_kv(x, kv_norm_g, w_dkv, ckv_norm_g, w_ukv, cos, sin):
    Bsz, S, _ = x.shape
    kv_in = rmsnorm(x, kv_norm_g)
    ckr = kv_in @ w_dkv
    c_kv, k_rope = ckr[..., :KV_LORA_RANK], ckr[..., KV_LORA_RANK:]
    c_kv = rmsnorm(c_kv, ckv_norm_g)
    kv = (c_kv @ w_ukv).reshape(Bsz, S, N_HEADS, QK_NOPE + V_HEAD)
    k_nope, v = kv[..., :QK_NOPE], kv[..., QK_NOPE:]
    k_rope = apply_rope(k_rope, cos, sin)
    return k_nope, k_rope, v


def mla_mixer(h, w_dq, cq_norm_g, w_uq, w_o, k_nope, k_rope, v, cos, sin):
    Bsz, S, _ = h.shape
    cq = rmsnorm(h @ w_dq, cq_norm_g)
    q = (cq @ w_uq).reshape(Bsz, S, N_HEADS, QK_NOPE + QK_ROPE)
    q_nope = q[..., :QK_NOPE]
    q_rope = apply_rope(q[..., QK_NOPE:], cos[:, :, None, :], sin[:, :, None, :])
    nq = S // QBLK
    qn = q_nope.reshape(Bsz, nq, QBLK, N_HEADS, QK_NOPE).transpose(1, 0, 2, 3, 4)
    qr = q_rope.reshape(Bsz, nq, QBLK, N_HEADS, QK_ROPE).transpose(1, 0, 2, 3, 4)
    key_chunk = jnp.arange(S) // CHUNK

    def one_block(args):
        qn_b, qr_b, i = args
        s = (jnp.einsum('bqhd,bkhd->bhqk', qn_b, k_nope)
             + jnp.einsum('bqhr,bkr->bhqk', qr_b, k_rope)).astype(jnp.float32) * ATTN_SCALE
        q_chunk = (i * QBLK + jnp.arange(QBLK)) // CHUNK
        mask = key_chunk[None, :] <= q_chunk[:, None]
        p = jax.nn.softmax(jnp.where(mask, s, -1e30), axis=-1).astype(v.dtype)
        return jnp.einsum('bhqk,bkhd->bqhd', p, v)

    out = lax.map(one_block, (qn, qr, jnp.arange(nq)))
    out = out.transpose(1, 0, 2, 3, 4).reshape(Bsz, S, N_HEADS * V_HEAD)
    return out @ w_o


def conv_ffn(h, w_gate, w_up, conv_w, conv_b, w_down):
    g = h @ w_gate
    g = lax.conv_general_dilated(g, conv_w[:, None, :], window_strides=(1,), padding=[(CONV_W - 1, 0)],
                                 dimension_numbers=('NWC', 'WIO', 'NWC'), feature_group_count=D_FF) + conv_b
    return (jax.nn.gelu(g) * (h @ w_up)) @ w_down


def setup_inputs(seed: int = 0) -> dict:
    key = jax.random.key(seed)
    ks = iter(jax.random.split(key, 40))
    f32 = jnp.float32
    nrm = lambda shape, scale: jax.random.normal(next(ks), shape, f32) * scale
    D, G, N, P, H = D_MODEL, SSM_GROUPS, SSM_STATE, SSM_GROUP, N_HEADS
    NA, NB = N_A_LAYERS, N_B_LAYERS
    x = jax.random.normal(next(ks), (BATCH, SEQ, D), f32)
    offset = jax.random.randint(next(ks), (BATCH,), 0, 64) * CHUNK
    positions = (offset[:, None] + jnp.arange(SEQ)[None, :]).astype(jnp.int32)
    norm_g = 1.0 + nrm((DEPTH, 4, D), 0.02)
    n_idx = jnp.arange(N, dtype=f32)
    ssm_a_re = -0.5 * (1.0 + nrm((NA, G, N), 0.02))
    ssm_a_im = math.pi * n_idx + nrm((NA, G, N), 0.01)
    ssm_log_dt = jax.random.uniform(next(ks), (NA, G), f32, math.log(DT_MIN), math.log(DT_MAX))
    ssm_b_re = nrm((NA, G, N, P), (2 * P) ** -0.5)
    ssm_b_im = nrm((NA, G, N, P), (2 * P) ** -0.5)
    ssm_c_re = nrm((NA, G, P, N), (2 * N) ** -0.5)
    ssm_c_im = nrm((NA, G, P, N), (2 * N) ** -0.5)
    ssm_d = nrm((NA, G, P), 1.0)
    ssm_w_glu = nrm((NA, D, 2 * D), D ** -0.5)
    ssm_b_glu = nrm((NA, 2 * D), 0.01)
    kv_norm_g = 1.0 + nrm((D,), 0.02)
    w_dkv = nrm((D, KV_LORA_RANK + QK_ROPE), D ** -0.5)
    ckv_norm_g = 1.0 + nrm((KV_LORA_RANK,), 0.02)
    w_ukv = nrm((KV_LORA_RANK, H * (QK_NOPE + V_HEAD)), KV_LORA_RANK ** -0.5)
    w_dq = nrm((NB, D, Q_LORA_RANK), D ** -0.5)
    cq_norm_g = 1.0 + nrm((NB, Q_LORA_RANK), 0.02)
    w_uq = nrm((NB, Q_LORA_RANK, H * (QK_NOPE + QK_ROPE)), Q_LORA_RANK ** -0.5)
    w_o = nrm((NB, H * V_HEAD, D), (H * V_HEAD) ** -0.5)
    ffn_w_gate = nrm((DEPTH, D, D_FF), D ** -0.5)
    ffn_w_up = nrm((DEPTH, D, D_FF), D ** -0.5)
    ffn_conv_w = nrm((DEPTH, CONV_W, D_FF), CONV_W ** -0.5)
    ffn_conv_b = nrm((DEPTH, D_FF), 0.01)
    ffn_w_down = nrm((DEPTH, D_FF, D), D_FF ** -0.5)
    return {"x": x, "positions": positions, "norm_g": norm_g,
            "ssm_a_re": ssm_a_re, "ssm_a_im": ssm_a_im, "ssm_log_dt": ssm_log_dt,
            "ssm_b_re": ssm_b_re, "ssm_b_im": ssm_b_im, "ssm_c_re": ssm_c_re, "ssm_c_im": ssm_c_im,
            "ssm_d": ssm_d, "ssm_w_glu": ssm_w_glu, "ssm_b_glu": ssm_b_glu,
            "kv_norm_g": kv_norm_g, "w_dkv": w_dkv, "ckv_norm_g": ckv_norm_g, "w_ukv": w_ukv,
            "w_dq": w_dq, "cq_norm_g": cq_norm_g, "w_uq": w_uq, "w_o": w_o,
            "ffn_w_gate": ffn_w_gate, "ffn_w_up": ffn_w_up, "ffn_conv_w": ffn_conv_w,
            "ffn_conv_b": ffn_conv_b, "ffn_w_down": ffn_w_down}


def reference(x, positions, norm_g, ssm_a_re, ssm_a_im, ssm_log_dt, ssm_b_re, ssm_b_im, ssm_c_re, ssm_c_im,
              ssm_d, ssm_w_glu, ssm_b_glu, kv_norm_g, w_dkv, ckv_norm_g, w_ukv, w_dq, cq_norm_g, w_uq, w_o,
              ffn_w_gate, ffn_w_up, ffn_conv_w, ffn_conv_b, ffn_w_down):
    cos, sin = rope_cos_sin(positions)
    for l in range(DEPTH):
        h = rmsnorm(x, norm_g[l, 0])
        if l < N_A_LAYERS:
            m = s5_mixer(h, ssm_a_re[l], ssm_a_im[l], ssm_log_dt[l], ssm_b_re[l], ssm_b_im[l],
                         ssm_c_re[l], ssm_c_im[l], ssm_d[l], ssm_w_glu[l], ssm_b_glu[l])
        else:
            if l == N_A_LAYERS:
                k_nope, k_rope, v = shared_kv(x, kv_norm_g, w_dkv, ckv_norm_g, w_ukv, cos, sin)
            j = l - N_A_LAYERS
            m = mla_mixer(h, w_dq[j], cq_norm_g[j], w_uq[j], w_o[j], k_nope, k_rope, v, cos, sin)
        x = x + rmsnorm(m, norm_g[l, 1])
        h = rmsnorm(x, norm_g[l, 2])
        f = conv_ffn(h, ffn_w_gate[l], ffn_w_up[l], ffn_conv_w[l], ffn_conv_b[l], ffn_w_down[l])
        x = x + rmsnorm(f, norm_g[l, 3])
    return x
```

```python
import functools
import math

import jax
import jax.numpy as jnp
from jax import lax
from jax.experimental import pallas as pl
from jax.experimental.pallas import tpu as pltpu

F32 = jnp.float32
BF16 = jnp.bfloat16

EPS = 1e-6
CHUNK = 64
QK_NOPE = 128
QK_ROPE = 64
V_HEAD = 128
ROPE_THETA = 10000.0
ATTN_SCALE = (QK_NOPE + QK_ROPE) ** -0.5
SSM_L = 16
S5_GROUP_BLOCK = 8
S5_ROW_PAD = 8
LANES = 128
NEG_INF = -1e30
V7X_VMEM_BYTES = 64 * 1024 * 1024


def _cparams(semantics, vmem_mib):
    assert vmem_mib * 1024 * 1024 < V7X_VMEM_BYTES
    return pltpu.CompilerParams(dimension_semantics=semantics,
                                vmem_limit_bytes=vmem_mib * 1024 * 1024)


def _tile(n, pref):
    t = min(n, pref)
    assert n % t == 0, (n, pref)
    return t


def _rms(x, g):
    return x * lax.rsqrt(jnp.mean(x * x, axis=-1, keepdims=True) + EPS) * g


def _gelu(x):
    c = math.sqrt(2.0 / math.pi)
    return 0.5 * x * (1.0 + jnp.tanh(c * (x + 0.044715 * (x * x * x))))


def _dot(a, b):
    return jnp.dot(a, b, preferred_element_type=F32)


def _resident(shape):
    nd = len(shape)
    return pl.BlockSpec(shape, lambda *_: (0,) * nd, pipeline_mode=pl.Buffered(1))


def _norm_kernel(x_ref, g_ref, o_ref):
    o_ref[...] = _rms(x_ref[...], g_ref[...]).astype(o_ref.dtype)


def _norm_cast(x, g):
    t, d = x.shape
    tm = _tile(t, 1024)
    return pl.pallas_call(
        _norm_kernel,
        grid=(t // tm,),
        in_specs=[pl.BlockSpec((tm, d), lambda i: (i, 0)), _resident((1, d))],
        out_specs=pl.BlockSpec((tm, d), lambda i: (i, 0)),
        out_shape=jax.ShapeDtypeStruct((t, d), BF16),
        compiler_params=_cparams(("parallel",), 40),
        name="norm_cast",
    )(x, g.reshape(1, d))


def _s5_weights(a_re, a_im, log_dt, b_re, b_im, c_re, c_im, d_skip):
    hi = lax.Precision.HIGHEST
    g, n = a_re.shape
    p = b_re.shape[-1]
    el = SSM_L
    dt = jnp.exp(log_dt)[:, None]
    mag = jnp.exp(dt * a_re)
    ab_re = mag * jnp.cos(dt * a_im)
    ab_im = mag * jnp.sin(dt * a_im)
    den = a_re * a_re + a_im * a_im
    nr, ni = ab_re - 1.0, ab_im
    coef_re = ((nr * a_re + ni * a_im) / den)[..., None]
    coef_im = ((ni * a_re - nr * a_im) / den)[..., None]
    bb_re = coef_re * b_re - coef_im * b_im
    bb_im = coef_re * b_im + coef_im * b_re
    prs, pis = [jnp.ones_like(ab_re)], [jnp.zeros_like(ab_re)]
    for _ in range(el):
        r, i = prs[-1], pis[-1]
        prs.append(r * ab_re - i * ab_im)
        pis.append(r * ab_im + i * ab_re)
    pr, pi = jnp.stack(prs), jnp.stack(pis)
    mr = c_re[None] * pr[:, :, None, :] - c_im[None] * pi[:, :, None, :]
    mi = c_re[None] * pi[:, :, None, :] + c_im[None] * pr[:, :, None, :]
    cb = (jnp.einsum("dgpn,gnq->gdpq", mr[:el], bb_re, precision=hi)
          - jnp.einsum("dgpn,gnq->gdpq", mi[:el], bb_im, precision=hi))
    lag = jnp.arange(el)[None, :] - jnp.arange(el)[:, None]
    t5 = cb[:, jnp.clip(lag, 0, el - 1)]
    t5 = jnp.where((lag >= 0)[None, :, :, None, None], t5, 0.0)
    eye_l = jnp.eye(el, dtype=F32)[None, :, :, None, None]
    eye_p = jnp.eye(p, dtype=F32)[None, None, None, :, :]
    t5 = t5 + eye_l * eye_p * d_skip[:, None, None, :, None]
    wy = t5.transpose(0, 1, 4, 2, 3).reshape(g, el * p, el * p)
    prr, pir = pr[el - 1::-1][:el], pi[el - 1::-1][:el]
    q_re = prr[:, :, :, None] * bb_re[None] - pir[:, :, :, None] * bb_im[None]
    q_im = prr[:, :, :, None] * bb_im[None] + pir[:, :, :, None] * bb_re[None]
    q_re = q_re.transpose(1, 0, 3, 2).reshape(g, el * p, n)
    q_im = q_im.transpose(1, 0, 3, 2).reshape(g, el * p, n)
    wz = jnp.concatenate([q_re, q_im, q_im, q_re], axis=-1)
    r_re = mr[1:].transpose(1, 3, 0, 2).reshape(g, n, el * p)
    r_im = -mi[1:].transpose(1, 3, 0, 2).reshape(g, n, el * p)
    wr = jnp.concatenate([r_re, r_im], axis=1)
    al_re, al_im = pr[el], pi[el]
    ac = jnp.stack([jnp.concatenate([al_re, al_re], -1),
                    jnp.concatenate([-al_im, al_im], -1),
                    jnp.concatenate([al_im, -al_im], -1)])
    return wz.astype(BF16), wy.astype(BF16), wr.astype(BF16), ac


def _s5_kernel(u_ref, wz_ref, wy_ref, wr_ref, ac_ref, o_ref, z_scr, zs_scr, *, gb, nc, ns):
    stride = nc + S5_ROW_PAD
    for gi in range(gb):
        z = _dot(u_ref[gi], wz_ref[gi])
        z_scr[gi * stride:gi * stride + nc, :] = z[:, :ns]
        zs_scr[gi * stride:gi * stride + nc, :] = z[:, ns:]
    ar, ai, ais = ac_ref[0], ac_ref[1], ac_ref[2]

    def body(c, carry):
        h, hs = carry
        rows = pl.ds(c, gb, stride=stride)
        z = z_scr[rows, :]
        zs = zs_scr[rows, :]
        z_scr[rows, :] = h
        return z + ar * h + ai * hs, zs + ar * hs + ais * h

    h0 = jnp.zeros((gb, ns), F32)
    lax.fori_loop(0, nc, body, (h0, h0), unroll=8)
    for gi in range(gb):
        hin = z_scr[gi * stride:gi * stride + nc, :].astype(BF16)
        y = _dot(u_ref[gi], wy_ref[gi]) + _dot(hin, wr_ref[gi])
        o_ref[gi] = _gelu(y).astype(o_ref.dtype)


def _s5_scan(ut, wz, wy, wr, ac, nb):
    g, rows, lp = ut.shape
    nc = rows // nb
    ns = wr.shape[1]
    gb = _tile(g, S5_GROUP_BLOCK)
    kern = functools.partial(_s5_kernel, gb=gb, nc=nc, ns=ns)
    return pl.pallas_call(
        kern,
        grid=(g // gb, nb),
        in_specs=[pl.BlockSpec((gb, nc, lp), lambda i, b: (i, b, 0)),
                  pl.BlockSpec((gb, lp, 2 * ns), lambda i, b: (i, 0, 0)),
                  pl.BlockSpec((gb, lp, lp), lambda i, b: (i, 0, 0)),
                  pl.BlockSpec((gb, ns, lp), lambda i, b: (i, 0, 0)),
                  pl.BlockSpec((3, gb, ns), lambda i, b: (0, i, 0))],
        out_specs=pl.BlockSpec((gb, nc, lp), lambda i, b: (i, b, 0)),
        out_shape=jax.ShapeDtypeStruct((g, rows, lp), BF16),
        scratch_shapes=[pltpu.VMEM((gb * (nc + S5_ROW_PAD), ns), F32),
                        pltpu.VMEM((gb * (nc + S5_ROW_PAD), ns), F32)],
        compiler_params=_cparams(("parallel", "parallel"), 48),
        name="s5_scan",
    )(ut, wz, wy, wr, ac)


def _glu_kernel(y_ref, x_ref, w_ref, b_ref, g_ref, o_ref):
    d = x_ref.shape[1]
    y = y_ref[...]
    val = _dot(y, w_ref[:, :d]) + b_ref[:, :d]
    gate = _dot(y, w_ref[:, d:]) + b_ref[:, d:]
    m = val * jax.nn.sigmoid(gate)
    o_ref[...] = x_ref[...] + _rms(m, g_ref[...])


def _glu_residual(y, x, w, b, g):
    t, d = x.shape
    tm = _tile(t, 256)
    return pl.pallas_call(
        _glu_kernel,
        grid=(t // tm,),
        in_specs=[pl.BlockSpec((tm, d), lambda i: (i, 0)),
                  pl.BlockSpec((tm, d), lambda i: (i, 0)),
                  _resident((d, 2 * d)), _resident((1, 2 * d)), _resident((1, d))],
        out_specs=pl.BlockSpec((tm, d), lambda i: (i, 0)),
        out_shape=jax.ShapeDtypeStruct((t, d), F32),
        compiler_params=_cparams(("parallel",), 48),
        name="glu_residual",
    )(y, x, w, b.reshape(1, 2 * d), g.reshape(1, d))


FFN_HALO = 8


def _ffn_kernel(x_ref, xp_ref, g2_ref, wg_ref, wu_ref, cw_ref, cb_ref, wd_ref, g3_ref, o_ref,
                h_scr, acc_scr, *, tm, tiles_per_seq):
    i = pl.program_id(0)
    j = pl.program_id(1)

    @pl.when(j == 0)
    def _():
        h_scr[FFN_HALO:, :] = _rms(x_ref[...], g2_ref[...]).astype(BF16)
        keep = jnp.where(i % tiles_per_seq != 0, 1.0, 0.0)
        h_scr[:FFN_HALO, :] = (_rms(xp_ref[...], g2_ref[...]) * keep).astype(BF16)
        acc_scr[...] = jnp.zeros_like(acc_scr)

    ge = _dot(h_scr[...], wg_ref[...])
    up = _dot(h_scr[FFN_HALO:, :], wu_ref[...])
    cw = cw_ref[...]
    conv = (cw[2:3, :] * ge[FFN_HALO:, :] + cw[1:2, :] * ge[FFN_HALO - 1:FFN_HALO - 1 + tm, :]
            + cw[0:1, :] * ge[FFN_HALO - 2:FFN_HALO - 2 + tm, :] + cb_ref[...])
    act = (_gelu(conv) * up).astype(BF16)
    acc_scr[...] += _dot(act, wd_ref[...])

    @pl.when(j == pl.num_programs(1) - 1)
    def _():
        o_ref[...] = x_ref[...] + _rms(acc_scr[...], g3_ref[...])


def _ffn_residual(x, g2, wg, wu, cw, cb, wd, g3, seq):
    t, d = x.shape
    f = wg.shape[1]
    tm = _tile(seq, 512)
    tf = _tile(f, 512)
    hb = tm // FFN_HALO
    kern = functools.partial(_ffn_kernel, tm=tm, tiles_per_seq=seq // tm)
    return pl.pallas_call(
        kern,
        grid=(t // tm, f // tf),
        in_specs=[pl.BlockSpec((tm, d), lambda i, j: (i, 0)),
                  pl.BlockSpec((FFN_HALO, d), lambda i, j: (jnp.maximum(i * hb - 1, 0), 0)),
                  _resident((1, d)),
                  pl.BlockSpec((d, tf), lambda i, j: (0, j)),
                  pl.BlockSpec((d, tf), lambda i, j: (0, j)),
                  pl.BlockSpec((3, tf), lambda i, j: (0, j)),
                  pl.BlockSpec((1, tf), lambda i, j: (0, j)),
                  pl.BlockSpec((tf, d), lambda i, j: (j, 0)),
                  _resident((1, d))],
        out_specs=pl.BlockSpec((tm, d), lambda i, j: (i, 0)),
        out_shape=jax.ShapeDtypeStruct((t, d), F32),
        scratch_shapes=[pltpu.VMEM((FFN_HALO + tm, d), BF16), pltpu.VMEM((tm, d), F32)],
        compiler_params=_cparams(("parallel", "arbitrary"), 56),
        name="ffn_residual",
    )(x, x, g2.reshape(1, d), wg, wu, cw, cb.reshape(1, f), wd, g3.reshape(1, d))


def _rope_pair(v, cs):
    p = v * cs
    return p + pltpu.roll(p, QK_ROPE, axis=1)


def _kv_kernel(x_ref, g_ref, wd_ref, cg_ref, wu_ref, cs_ref, kv_ref, kr_ref, *, rank):
    h = _rms(x_ref[...], g_ref[...]).astype(BF16)
    ckr = _dot(h, wd_ref[...])
    c = _rms(ckr[:, :rank], cg_ref[...]).astype(BF16)
    kv_ref[...] = _dot(c, wu_ref[...]).astype(kv_ref.dtype)
    r = _rope_pair(ckr[:, rank:], cs_ref[...])
    lane = lax.broadcasted_iota(jnp.int32, r.shape, 1)
    kr_ref[...] = jnp.where(lane < QK_ROPE, r, 0.0).astype(kr_ref.dtype)


def _kv_proj(x, g, wd, cg, wu, cs):
    t, d = x.shape
    rank = wu.shape[0]
    nkv = wu.shape[1]
    tm = _tile(t, 512)
    kern = functools.partial(_kv_kernel, rank=rank)
    return pl.pallas_call(
        kern,
        grid=(t // tm,),
        in_specs=[pl.BlockSpec((tm, d), lambda i: (i, 0)),
                  _resident((1, d)), _resident((d, rank + LANES)), _resident((1, rank)),
                  _resident((rank, nkv)),
                  pl.BlockSpec((tm, LANES), lambda i: (i, 0))],
        out_specs=[pl.BlockSpec((tm, nkv), lambda i: (i, 0)),
                   pl.BlockSpec((tm, LANES), lambda i: (i, 0))],
        out_shape=[jax.ShapeDtypeStruct((t, nkv), BF16), jax.ShapeDtypeStruct((t, LANES), BF16)],
        compiler_params=_cparams(("parallel",), 48),
        name="kv_proj",
    )(x, g.reshape(1, d), wd, cg.reshape(1, rank), wu, cs)


def _q_kernel(x_ref, g_ref, wd_ref, cg_ref, wu_ref, cs_ref, q_ref, *, heads):
    h = _rms(x_ref[...], g_ref[...]).astype(BF16)
    cq = _rms(_dot(h, wd_ref[...]), cg_ref[...]).astype(BF16)
    cs = cs_ref[...]
    w = 2 * LANES
    for hd in range(heads):
        qh = _dot(cq, wu_ref[:, hd * w:(hd + 1) * w])
        q_ref[:, hd * w:hd * w + LANES] = (qh[:, :LANES] * ATTN_SCALE).astype(q_ref.dtype)
        r = _rope_pair(qh[:, LANES:], cs)
        q_ref[:, hd * w + LANES:(hd + 1) * w] = (r * ATTN_SCALE).astype(q_ref.dtype)


def _q_proj(x, g, wd, cg, wu, cs, heads):
    t, d = x.shape
    rank = wd.shape[1]
    nq = wu.shape[1]
    tm = _tile(t, 512)
    kern = functools.partial(_q_kernel, heads=heads)
    return pl.pallas_call(
        kern,
        grid=(t // tm,),
        in_specs=[pl.BlockSpec((tm, d), lambda i: (i, 0)),
                  _resident((1, d)), _resident((d, rank)), _resident((1, rank)),
                  _resident((rank, nq)),
                  pl.BlockSpec((tm, LANES), lambda i: (i, 0))],
        out_specs=pl.BlockSpec((tm, nq), lambda i: (i, 0)),
        out_shape=jax.ShapeDtypeStruct((t, nq), BF16),
        compiler_params=_cparams(("parallel",), 48),
        name="q_proj",
    )(x, g.reshape(1, d), wd, cg.reshape(1, rank), wu, cs)


def _attn_kernel(q_ref, kn_ref, kr_ref, v_ref, o_ref, m_scr, l_scr, acc_scr, *, tq):
    i = pl.program_id(2)
    q = q_ref[...]
    m_scr[...] = jnp.full_like(m_scr, NEG_INF)
    l_scr[...] = jnp.zeros_like(l_scr)
    acc_scr[...] = jnp.zeros_like(acc_scr)

    def step(j, masked):
        rows = pl.ds(pl.multiple_of(j * tq, tq), tq)
        k = jnp.concatenate([kn_ref[rows, :], kr_ref[rows, :]], axis=1)
        s = lax.dot_general(q, k, (((1,), (1,)), ((), ())), preferred_element_type=F32)
        if masked:
            qc = lax.broadcasted_iota(jnp.int32, s.shape, 0) // CHUNK
            kc = lax.broadcasted_iota(jnp.int32, s.shape, 1) // CHUNK
            s = jnp.where(kc <= qc, s, NEG_INF)
        m_prev = m_scr[...]
        m_new = jnp.maximum(m_prev, jnp.max(s, axis=-1, keepdims=True))
        alpha = jnp.exp(m_prev - m_new)
        p = jnp.exp(s - m_new)
        l_scr[...] = alpha * l_scr[...] + jnp.sum(p, axis=-1, keepdims=True)
        acc_scr[...] = alpha * acc_scr[...] + _dot(p.astype(BF16), v_ref[rows, :])
        m_scr[...] = m_new

    def full_step(j, carry):
        step(j, False)
        return carry

    lax.fori_loop(0, i, full_step, 0)
    step(i, True)
    o_ref[...] = (acc_scr[...] / l_scr[...]).astype(o_ref.dtype)


def _attention(q, kv, kr, nb, seq, heads):
    t = q.shape[0]
    tq = _tile(seq, 512)
    assert tq % CHUNK == 0
    nq = seq // tq
    kern = functools.partial(_attn_kernel, tq=tq)
    return pl.pallas_call(
        kern,
        grid=(nb, heads, nq),
        in_specs=[pl.BlockSpec((tq, 2 * LANES), lambda b, h, i: (b * nq + i, h)),
                  pl.BlockSpec((seq, QK_NOPE), lambda b, h, i: (b, 2 * h)),
                  pl.BlockSpec((seq, LANES), lambda b, h, i: (b, 0)),
                  pl.BlockSpec((seq, V_HEAD), lambda b, h, i: (b, 2 * h + 1))],
        out_specs=pl.BlockSpec((tq, V_HEAD), lambda b, h, i: (b * nq + i, h)),
        out_shape=jax.ShapeDtypeStruct((t, heads * V_HEAD), BF16),
        scratch_shapes=[pltpu.VMEM((tq, 1), F32), pltpu.VMEM((tq, 1), F32),
                        pltpu.VMEM((tq, V_HEAD), F32)],
        compiler_params=_cparams(("parallel", "parallel", "arbitrary"), 48),
        name="attention",
    )(q, kv, kr, kv)


def _oproj_kernel(a_ref, x_ref, w_ref, g_ref, o_ref):
    o_ref[...] = x_ref[...] + _rms(_dot(a_ref[...], w_ref[...]), g_ref[...])


def _oproj_residual(a, x, w, g):
    t, d = x.shape
    k = a.shape[1]
    tm = _tile(t, 512)
    return pl.pallas_call(
        _oproj_kernel,
        grid=(t // tm,),
        in_specs=[pl.BlockSpec((tm, k), lambda i: (i, 0)),
                  pl.BlockSpec((tm, d), lambda i: (i, 0)),
                  _resident((k, d)), _resident((1, d))],
        out_specs=pl.BlockSpec((tm, d), lambda i: (i, 0)),
        out_shape=jax.ShapeDtypeStruct((t, d), F32),
        compiler_params=_cparams(("parallel",), 48),
        name="oproj_residual",
    )(a, x, w, g.reshape(1, d))


def _swap_halves(w):
    half = w.shape[-1] // 2
    return jnp.concatenate([w[..., half:], w[..., :half]], axis=-1)


def kernel(x, positions, norm_g, ssm_a_re, ssm_a_im, ssm_log_dt, ssm_b_re, ssm_b_im, ssm_c_re, ssm_c_im, ssm_d, ssm_w_glu, ssm_b_glu, kv_norm_g, w_dkv, ckv_norm_g, w_ukv, w_dq, cq_norm_g, w_uq, w_o, ffn_w_gate, ffn_w_up, ffn_conv_w, ffn_conv_b, ffn_w_down):
    nb, seq, d = x.shape
    t = nb * seq
    n_a = ssm_a_re.shape[0]
    n_b = w_dq.shape[0]
    xs = x.reshape(t, d).astype(F32)

    def ffn(xs, l):
        return _ffn_residual(xs, norm_g[l, 2], ffn_w_gate[l].astype(BF16), ffn_w_up[l].astype(BF16),
                             ffn_conv_w[l], ffn_conv_b[l], ffn_w_down[l].astype(BF16), norm_g[l, 3], seq)

    for l in range(n_a):
        g, _ = ssm_a_re[l].shape
        p = d // g
        nc = seq // SSM_L
        wz, wy, wr, ac = _s5_weights(ssm_a_re[l], ssm_a_im[l], ssm_log_dt[l], ssm_b_re[l], ssm_b_im[l],
                                     ssm_c_re[l], ssm_c_im[l], ssm_d[l])
        u = _norm_cast(xs, norm_g[l, 0])
        ut = u.reshape(nb, nc, SSM_L, g, p).transpose(3, 0, 1, 2, 4).reshape(g, nb * nc, SSM_L * p)
        yt = _s5_scan(ut, wz, wy, wr, ac, nb)
        y = yt.reshape(g, nb, nc, SSM_L, p).transpose(1, 2, 3, 0, 4).reshape(t, d)
        xs = _glu_residual(y, xs, ssm_w_glu[l].astype(BF16), ssm_b_glu[l], norm_g[l, 1])
        xs = ffn(xs, l)

    heads = w_o.shape[1] // V_HEAD
    rank_kv = w_ukv.shape[0]
    freqs = ROPE_THETA ** (-jnp.arange(0, QK_ROPE, 2, dtype=F32) / QK_ROPE)
    ang = positions.reshape(t).astype(F32)[:, None] * freqs
    cos, sin = jnp.cos(ang), jnp.sin(ang)
    cs = jnp.concatenate([cos, cos, -sin, sin], axis=-1)
    wd_kv = jnp.concatenate([w_dkv, _swap_halves(w_dkv[:, rank_kv:])], axis=1).astype(BF16)
    kv, kr = _kv_proj(xs, kv_norm_g, wd_kv, ckv_norm_g, w_ukv.astype(BF16), cs)
    for jl in range(n_b):
        l = n_a + jl
        wq = w_uq[jl].reshape(-1, heads, QK_NOPE + QK_ROPE)
        wq = jnp.concatenate([wq, _swap_halves(wq[..., QK_NOPE:])], axis=-1)
        wq = wq.reshape(-1, heads * 2 * LANES).astype(BF16)
        q = _q_proj(xs, norm_g[l, 0], w_dq[jl].astype(BF16), cq_norm_g[jl], wq, cs, heads)
        a = _attention(q, kv, kr, nb, seq, heads)
        xs = _oproj_residual(a, xs, w_o[jl].astype(BF16), norm_g[l, 1])
        xs = ffn(xs, l)
    return xs.reshape(nb, seq, d).astype(x.dtype)
```

```python
import functools
import math

import jax
import jax.numpy as jnp
from jax import lax
from jax.experimental import pallas as pl
from jax.experimental.pallas import tpu as pltpu

F32 = jnp.float32
BF16 = jnp.bfloat16

EPS = 1e-6
CHUNK = 64
QK_NOPE = 128
QK_ROPE = 64
V_HEAD = 128
ROPE_THETA = 10000.0
ATTN_SCALE = (QK_NOPE + QK_ROPE) ** -0.5
SSM_L = 16
S5_GROUP_BLOCK = 8
S5_ROW_PAD = 8
LANES = 128
NEG_INF = -1e30
V7X_VMEM_BYTES = 64 * 1024 * 1024


def _cparams(semantics, vmem_mib):
    assert vmem_mib * 1024 * 1024 < V7X_VMEM_BYTES
    return pltpu.CompilerParams(dimension_semantics=semantics,
                                vmem_limit_bytes=vmem_mib * 1024 * 1024)


def _tile(n, pref):
    t = min(n, pref)
    assert n % t == 0, (n, pref)
    return t


def _rms(x, g):
    return x * lax.rsqrt(jnp.mean(x * x, axis=-1, keepdims=True) + EPS) * g


def _gelu(x):
    c = math.sqrt(2.0 / math.pi)
    return 0.5 * x * (1.0 + jnp.tanh(c * (x + 0.044715 * (x * x * x))))


def _dot(a, b):
    return jnp.dot(a, b, preferred_element_type=F32)


def _resident(shape):
    nd = len(shape)
    return pl.BlockSpec(shape, lambda *_: (0,) * nd, pipeline_mode=pl.Buffered(1))


def _norm_kernel(x_ref, g_ref, o_ref):
    o_ref[...] = _rms(x_ref[...], g_ref[...]).astype(o_ref.dtype)


def _norm_cast(x, g):
    t, d = x.shape
    tm = _tile(t, 1024)
    return pl.pallas_call(
        _norm_kernel,
        grid=(t // tm,),
        in_specs=[pl.BlockSpec((tm, d), lambda i: (i, 0)), _resident((1, d))],
        out_specs=pl.BlockSpec((tm, d), lambda i: (i, 0)),
        out_shape=jax.ShapeDtypeStruct((t, d), BF16),
        compiler_params=_cparams(("parallel",), 40),
        name="norm_cast",
    )(x, g.reshape(1, d))


def _s5_weights(a_re, a_im, log_dt, b_re, b_im, c_re, c_im, d_skip):
    hi = lax.Precision.HIGHEST
    g, n = a_re.shape
    p = b_re.shape[-1]
    el = SSM_L
    dt = jnp.exp(log_dt)[:, None]
    mag = jnp.exp(dt * a_re)
    ab_re = mag * jnp.cos(dt * a_im)
    ab_im = mag * jnp.sin(dt * a_im)
    den = a_re * a_re + a_im * a_im
    nr, ni = ab_re - 1.0, ab_im
    coef_re = ((nr * a_re + ni * a_im) / den)[..., None]
    coef_im = ((ni * a_re - nr * a_im) / den)[..., None]
    bb_re = coef_re * b_re - coef_im * b_im
    bb_im = coef_re * b_im + coef_im * b_re
    prs, pis = [jnp.ones_like(ab_re)], [jnp.zeros_like(ab_re)]
    for _ in range(el):
        r, i = prs[-1], pis[-1]
        prs.append(r * ab_re - i * ab_im)
        pis.append(r * ab_im + i * ab_re)
    pr, pi = jnp.stack(prs), jnp.stack(pis)
    mr = c_re[None] * pr[:, :, None, :] - c_im[None] * pi[:, :, None, :]
    mi = c_re[None] * pi[:, :, None, :] + c_im[None] * pr[:, :, None, :]
    cb = (jnp.einsum("dgpn,gnq->gdpq", mr[:el], bb_re, precision=hi)
          - jnp.einsum("dgpn,gnq->gdpq", mi[:el], bb_im, precision=hi))
    lag = jnp.arange(el)[None, :] - jnp.arange(el)[:, None]
    t5 = cb[:, jnp.clip(lag, 0, el - 1)]
    t5 = jnp.where((lag >= 0)[None, :, :, None, None], t5, 0.0)
    eye_l = jnp.eye(el, dtype=F32)[None, :, :, None, None]
    eye_p = jnp.eye(p, dtype=F32)[None, None, None, :, :]
    t5 = t5 + eye_l * eye_p * d_skip[:, None, None, :, None]
    wy = t5.transpose(0, 1, 4, 2, 3).reshape(g, el * p, el * p)
    prr, pir = pr[el - 1::-1][:el], pi[el - 1::-1][:el]
    q_re = prr[:, :, :, None] * bb_re[None] - pir[:, :, :, None] * bb_im[None]
    q_im = prr[:, :, :, None] * bb_im[None] + pir[:, :, :, None] * bb_re[None]
    q_re = q_re.transpose(1, 0, 3, 2).reshape(g, el * p, n)
    q_im = q_im.transpose(1, 0, 3, 2).reshape(g, el * p, n)
    wz = jnp.concatenate([q_re, q_im, q_im, q_re], axis=-1)
    r_re = mr[1:].transpose(1, 3, 0, 2).reshape(g, n, el * p)
    r_im = -mi[1:].transpose(1, 3, 0, 2).reshape(g, n, el * p)
    wr = jnp.concatenate([r_re, r_im], axis=1)
    al_re, al_im = pr[el], pi[el]
    ac = jnp.stack([jnp.concatenate([al_re, al_re], -1),
                    jnp.concatenate([-al_im, al_im], -1),
                    jnp.concatenate([al_im, -al_im], -1)])
    return wz.astype(BF16), wy.astype(BF16), wr.astype(BF16), ac


def _s5_kernel(u_ref, wz_ref, wy_ref, wr_ref, ac_ref, o_ref, z_scr, zs_scr, *, gb, nc, ns):
    stride = nc + S5_ROW_PAD
    for gi in range(gb):
        z = _dot(u_ref[gi], wz_ref[gi])
        z_scr[gi * stride:gi * stride + nc, :] = z[:, :ns]
        zs_scr[gi * stride:gi * stride + nc, :] = z[:, ns:]
    ar, ai, ais = ac_ref[0], ac_ref[1], ac_ref[2]

    def body(c, carry):
        h, hs = carry
        rows = pl.ds(c, gb, stride=stride)
        z = z_scr[rows, :]
        zs = zs_scr[rows, :]
        z_scr[rows, :] = h
        return z + ar * h + ai * hs, zs + ar * hs + ais * h

    h0 = jnp.zeros((gb, ns), F32)
    lax.fori_loop(0, nc, body, (h0, h0), unroll=8)
    for gi in range(gb):
        hin = z_scr[gi * stride:gi * stride + nc, :].astype(BF16)
        y = _dot(u_ref[gi], wy_ref[gi]) + _dot(hin, wr_ref[gi])
        o_ref[gi] = _gelu(y).astype(o_ref.dtype)


def _s5_scan(ut, wz, wy, wr, ac, nb):
    g, rows, lp = ut.shape
    nc = rows // nb
    ns = wr.shape[1]
    gb = _tile(g, S5_GROUP_BLOCK)
    kern = functools.partial(_s5_kernel, gb=gb, nc=nc, ns=ns)
    return pl.pallas_call(
        kern,
        grid=(g // gb, nb),
        in_specs=[pl.BlockSpec((gb, nc, lp), lambda i, b: (i, b, 0)),
                  pl.BlockSpec((gb, lp, 2 * ns), lambda i, b: (i, 0, 0)),
                  pl.BlockSpec((gb, lp, lp), lambda i, b: (i, 0, 0)),
                  pl.BlockSpec((gb, ns, lp), lambda i, b: (i, 0, 0)),
                  pl.BlockSpec((3, gb, ns), lambda i, b: (0, i, 0))],
        out_specs=pl.BlockSpec((gb, nc, lp), lambda i, b: (i, b, 0)),
        out_shape=jax.ShapeDtypeStruct((g, rows, lp), BF16),
        scratch_shapes=[pltpu.VMEM((gb * (nc + S5_ROW_PAD), ns), F32),
                        pltpu.VMEM((gb * (nc + S5_ROW_PAD), ns), F32)],
        compiler_params=_cparams(("parallel", "parallel"), 48),
        name="s5_scan",
    )(ut, wz, wy, wr, ac)


def _glu_kernel(y_ref, x_ref, w_ref, b_ref, g_ref, o_ref):
    d = x_ref.shape[1]
    y = y_ref[...]
    val = _dot(y, w_ref[:, :d]) + b_ref[:, :d]
    gate = _dot(y, w_ref[:, d:]) + b_ref[:, d:]
    m = val * jax.nn.sigmoid(gate)
    o_ref[...] = x_ref[...] + _rms(m, g_ref[...])


def _glu_residual(y, x, w, b, g):
    t, d = x.shape
    tm = _tile(t, 256)
    return pl.pallas_call(
        _glu_kernel,
        grid=(t // tm,),
        in_specs=[pl.BlockSpec((tm, d), lambda i: (i, 0)),
                  pl.BlockSpec((tm, d), lambda i: (i, 0)),
                  _resident((d, 2 * d)), _resident((1, 2 * d)), _resident((1, d))],
        out_specs=pl.BlockSpec((tm, d), lambda i: (i, 0)),
        out_shape=jax.ShapeDtypeStruct((t, d), F32),
        compiler_params=_cparams(("parallel",), 48),
        name="glu_residual",
    )(y, x, w, b.reshape(1, 2 * d), g.reshape(1, d))


FFN_HALO = 8


def _ffn_kernel(x_ref, xp_ref, g2_ref, wg_ref, wu_ref, cw_ref, cb_ref, wd_ref, g3_ref, o_ref,
                h_scr, acc_scr, *, tm, tiles_per_seq):
    i = pl.program_id(0)
    j = pl.program_id(1)

    @pl.when(j == 0)
    def _():
        h_scr[FFN_HALO:, :] = _rms(x_ref[...], g2_ref[...]).astype(BF16)
        keep = jnp.where(i % tiles_per_seq != 0, 1.0, 0.0)
        h_scr[:FFN_HALO, :] = (_rms(xp_ref[...], g2_ref[...]) * keep).astype(BF16)
        acc_scr[...] = jnp.zeros_like(acc_scr)

    ge = _dot(h_scr[...], wg_ref[...])
    up = _dot(h_scr[FFN_HALO:, :], wu_ref[...])
    cw = cw_ref[...]
    conv = (cw[2:3, :] * ge[FFN_HALO:, :] + cw[1:2, :] * ge[FFN_HALO - 1:FFN_HALO - 1 + tm, :]
            + cw[0:1, :] * ge[FFN_HALO - 2:FFN_HALO - 2 + tm, :] + cb_ref[...])
    act = (_gelu(conv) * up).astype(BF16)
    acc_scr[...] += _dot(act, wd_ref[...])

    @pl.when(j == pl.num_programs(1) - 1)
    def _():
        o_ref[...] = x_ref[...] + _rms(acc_scr[...], g3_ref[...])


def _ffn_residual(x, g2, wg, wu, cw, cb, wd, g3, seq):
    t, d = x.shape
    f = wg.shape[1]
    tm = _tile(seq, 512)
    tf = _tile(f, 512)
    hb = tm // FFN_HALO
    kern = functools.partial(_ffn_kernel, tm=tm, tiles_per_seq=seq // tm)
    return pl.pallas_call(
        kern,
        grid=(t // tm, f // tf),
        in_specs=[pl.BlockSpec((tm, d), lambda i, j: (i, 0)),
                  pl.BlockSpec((FFN_HALO, d), lambda i, j: (jnp.maximum(i * hb - 1, 0), 0)),
                  _resident((1, d)),
                  pl.BlockSpec((d, tf), lambda i, j: (0, j)),
                  pl.BlockSpec((d, tf), lambda i, j: (0, j)),
                  pl.BlockSpec((3, tf), lambda i, j: (0, j)),
                  pl.BlockSpec((1, tf), lambda i, j: (0, j)),
                  pl.BlockSpec((tf, d), lambda i, j: (j, 0)),
                  _resident((1, d))],
        out_specs=pl.BlockSpec((tm, d), lambda i, j: (i, 0)),
        out_shape=jax.ShapeDtypeStruct((t, d), F32),
        scratch_shapes=[pltpu.VMEM((FFN_HALO + tm, d), BF16), pltpu.VMEM((tm, d), F32)],
        compiler_params=_cparams(("parallel", "arbitrary"), 56),
        name="ffn_residual",
    )(x, x, g2.reshape(1, d), wg, wu, cw, cb.reshape(1, f), wd, g3.reshape(1, d))


ATTN_TILE = 512
_NT = (((1,), (1,)), ((), ()))


def _rope_pair(v, cs):
    p = v * cs
    return p + pltpu.roll(p, QK_ROPE, axis=1)


def _kv_kernel(x_ref, g_ref, wd_ref, cg_ref, wk_ref, wvt_ref, cs_ref, kn_ref, kr_ref, vt_ref, *, rank):
    h = _rms(x_ref[...], g_ref[...]).astype(BF16)
    ckr = _dot(h, wd_ref[...])
    c = _rms(ckr[:, :rank], cg_ref[...]).astype(BF16)
    kn_ref[...] = _dot(c, wk_ref[...]).astype(kn_ref.dtype)
    vt_ref[0] = lax.dot_general(wvt_ref[...], c, _NT, preferred_element_type=F32).astype(vt_ref.dtype)
    r = _rope_pair(ckr[:, rank:], cs_ref[...])
    lane = lax.broadcasted_iota(jnp.int32, r.shape, 1)
    kr_ref[...] = jnp.where(lane < QK_ROPE, r, 0.0).astype(kr_ref.dtype)


def _kv_proj(x, g, wd, cg, wk, wvt, cs, tm):
    t, d = x.shape
    rank = wk.shape[0]
    nk = wk.shape[1]
    nv = wvt.shape[0]
    kern = functools.partial(_kv_kernel, rank=rank)
    return pl.pallas_call(
        kern,
        grid=(t // tm,),
        in_specs=[pl.BlockSpec((tm, d), lambda i: (i, 0)),
                  _resident((1, d)), _resident((d, rank + LANES)), _resident((1, rank)),
                  _resident((rank, nk)), _resident((nv, rank)),
                  pl.BlockSpec((tm, LANES), lambda i: (i, 0))],
        out_specs=[pl.BlockSpec((tm, nk), lambda i: (i, 0)),
                   pl.BlockSpec((tm, LANES), lambda i: (i, 0)),
                   pl.BlockSpec((1, nv, tm), lambda i: (i, 0, 0))],
        out_shape=[jax.ShapeDtypeStruct((t, nk), BF16), jax.ShapeDtypeStruct((t, LANES), BF16),
                   jax.ShapeDtypeStruct((t // tm, nv, tm), BF16)],
        compiler_params=_cparams(("parallel",), 48),
        name="kv_proj",
    )(x, g.reshape(1, d), wd, cg.reshape(1, rank), wk, wvt, cs)


def _q_kernel(x_ref, g_ref, wd_ref, cg_ref, wut_ref, cst_ref, q_ref, *, heads):
    h = _rms(x_ref[...], g_ref[...]).astype(BF16)
    cq = _rms(_dot(h, wd_ref[...]), cg_ref[...]).astype(BF16)
    cst = cst_ref[...]
    w = 2 * LANES
    for hd in range(heads):
        qh = lax.dot_general(wut_ref[hd * w:(hd + 1) * w, :], cq, _NT, preferred_element_type=F32)
        q_ref[0, hd * w:hd * w + LANES, :] = (qh[:LANES] * ATTN_SCALE).astype(q_ref.dtype)
        p = qh[LANES:] * cst
        r = ((p[:QK_ROPE] + p[QK_ROPE:]) * ATTN_SCALE).astype(q_ref.dtype)
        q_ref[0, hd * w + LANES:(hd + 1) * w, :] = jnp.concatenate([r, r], axis=0)


def _q_proj(x, g, wd, cg, wut, cst, heads, tm):
    t, d = x.shape
    rank = wd.shape[1]
    nq = wut.shape[0]
    kern = functools.partial(_q_kernel, heads=heads)
    return pl.pallas_call(
        kern,
        grid=(t // tm,),
        in_specs=[pl.BlockSpec((tm, d), lambda i: (i, 0)),
                  _resident((1, d)), _resident((d, rank)), _resident((1, rank)),
                  _resident((nq, rank)),
                  pl.BlockSpec((LANES, tm), lambda i: (0, i))],
        out_specs=pl.BlockSpec((1, nq, tm), lambda i: (i, 0, 0)),
        out_shape=jax.ShapeDtypeStruct((t // tm, nq, tm), BF16),
        compiler_params=_cparams(("parallel",), 48),
        name="q_proj",
    )(x, g.reshape(1, d), wd, cg.reshape(1, rank), wut, cst)


def _attn_kernel(qt_ref, kn_ref, kr_ref, vt_ref, o_ref, m_scr, l_scr, acc_scr, *, tile):
    i = pl.program_id(2)
    qt = qt_ref[0]
    m_scr[...] = jnp.full_like(m_scr, NEG_INF)
    l_scr[...] = jnp.zeros_like(l_scr)
    acc_scr[...] = jnp.zeros_like(acc_scr)

    def step(j, masked):
        rows = pl.ds(pl.multiple_of(j * tile, tile), tile)
        k = jnp.concatenate([kn_ref[rows, :], kr_ref[rows, :]], axis=1)
        s = _dot(k, qt)
        if masked:
            kc = lax.broadcasted_iota(jnp.int32, s.shape, 0) // CHUNK
            qc = lax.broadcasted_iota(jnp.int32, s.shape, 1) // CHUNK
            s = jnp.where(kc <= qc, s, NEG_INF)
        m_prev = m_scr[...]
        m_new = jnp.maximum(m_prev, jnp.max(s, axis=0, keepdims=True))
        alpha = jnp.exp(m_prev - m_new)
        p = jnp.exp(s - m_new)
        l_scr[...] = alpha * l_scr[...] + jnp.sum(p, axis=0, keepdims=True)
        acc_scr[...] = alpha * acc_scr[...] + _dot(vt_ref[j], p.astype(BF16))
        m_scr[...] = m_new

    def full_step(j, carry):
        step(j, False)
        return carry

    lax.fori_loop(0, i, full_step, 0)
    step(i, True)
    o_ref[...] = (acc_scr[...] / l_scr[...]).T.astype(o_ref.dtype)


def _attention(qt, kn, kr, vt, nb, seq, heads):
    tile = qt.shape[2]
    assert tile % CHUNK == 0 and seq % tile == 0
    nq = seq // tile
    t = nb * seq
    kern = functools.partial(_attn_kernel, tile=tile)
    return pl.pallas_call(
        kern,
        grid=(nb, heads, nq),
        in_specs=[pl.BlockSpec((1, 2 * LANES, tile), lambda b, h, i: (b * nq + i, h, 0)),
                  pl.BlockSpec((seq, QK_NOPE), lambda b, h, i: (b, h)),
                  pl.BlockSpec((seq, LANES), lambda b, h, i: (b, 0)),
                  pl.BlockSpec((nq, V_HEAD, tile), lambda b, h, i: (b, h, 0))],
        out_specs=pl.BlockSpec((tile, V_HEAD), lambda b, h, i: (b * nq + i, h)),
        out_shape=jax.ShapeDtypeStruct((t, heads * V_HEAD), BF16),
        scratch_shapes=[pltpu.VMEM((1, tile), F32), pltpu.VMEM((1, tile), F32),
                        pltpu.VMEM((V_HEAD, tile), F32)],
        compiler_params=_cparams(("parallel", "parallel", "arbitrary"), 48),
        name="attention",
    )(qt, kn, kr, vt)


def _oproj_kernel(a_ref, x_ref, w_ref, g_ref, o_ref):
    o_ref[...] = x_ref[...] + _rms(_dot(a_ref[...], w_ref[...]), g_ref[...])


def _oproj_residual(a, x, w, g):
    t, d = x.shape
    k = a.shape[1]
    tm = _tile(t, 512)
    return pl.pallas_call(
        _oproj_kernel,
        grid=(t // tm,),
        in_specs=[pl.BlockSpec((tm, k), lambda i: (i, 0)),
                  pl.BlockSpec((tm, d), lambda i: (i, 0)),
                  _resident((k, d)), _resident((1, d))],
        out_specs=pl.BlockSpec((tm, d), lambda i: (i, 0)),
        out_shape=jax.ShapeDtypeStruct((t, d), F32),
        compiler_params=_cparams(("parallel",), 48),
        name="oproj_residual",
    )(a, x, w, g.reshape(1, d))


def _swap_halves(w):
    half = w.shape[-1] // 2
    return jnp.concatenate([w[..., half:], w[..., :half]], axis=-1)


def kernel(x, positions, norm_g, ssm_a_re, ssm_a_im, ssm_log_dt, ssm_b_re, ssm_b_im, ssm_c_re, ssm_c_im, ssm_d, ssm_w_glu, ssm_b_glu, kv_norm_g, w_dkv, ckv_norm_g, w_ukv, w_dq, cq_norm_g, w_uq, w_o, ffn_w_gate, ffn_w_up, ffn_conv_w, ffn_conv_b, ffn_w_down):
    nb, seq, d = x.shape
    t = nb * seq
    n_a = ssm_a_re.shape[0]
    n_b = w_dq.shape[0]
    xs = x.reshape(t, d).astype(F32)

    def ffn(xs, l):
        return _ffn_residual(xs, norm_g[l, 2], ffn_w_gate[l].astype(BF16), ffn_w_up[l].astype(BF16),
                             ffn_conv_w[l], ffn_conv_b[l], ffn_w_down[l].astype(BF16), norm_g[l, 3], seq)

    for l in range(n_a):
        g, _ = ssm_a_re[l].shape
        p = d // g
        nc = seq // SSM_L
        wz, wy, wr, ac = _s5_weights(ssm_a_re[l], ssm_a_im[l], ssm_log_dt[l], ssm_b_re[l], ssm_b_im[l],
                                     ssm_c_re[l], ssm_c_im[l], ssm_d[l])
        u = _norm_cast(xs, norm_g[l, 0])
        ut = u.reshape(nb, nc, SSM_L, g, p).transpose(3, 0, 1, 2, 4).reshape(g, nb * nc, SSM_L * p)
        yt = _s5_scan(ut, wz, wy, wr, ac, nb)
        y = yt.reshape(g, nb, nc, SSM_L, p).transpose(1, 2, 3, 0, 4).reshape(t, d)
        xs = _glu_residual(y, xs, ssm_w_glu[l].astype(BF16), ssm_b_glu[l], norm_g[l, 1])
        xs = ffn(xs, l)

    heads = w_o.shape[1] // V_HEAD
    rank_kv = w_ukv.shape[0]
    tile = _tile(seq, ATTN_TILE)
    freqs = ROPE_THETA ** (-jnp.arange(0, QK_ROPE, 2, dtype=F32) / QK_ROPE)
    ang = positions.reshape(t).astype(F32)[:, None] * freqs
    cos, sin = jnp.cos(ang), jnp.sin(ang)
    cs = jnp.concatenate([cos, cos, -sin, sin], axis=-1)
    wd_kv = jnp.concatenate([w_dkv, _swap_halves(w_dkv[:, rank_kv:])], axis=1).astype(BF16)
    w_kv = w_ukv.reshape(rank_kv, heads, QK_NOPE + V_HEAD)
    w_k = w_kv[..., :QK_NOPE].reshape(rank_kv, heads * QK_NOPE).astype(BF16)
    w_vt = w_kv[..., QK_NOPE:].reshape(rank_kv, heads * V_HEAD).T.astype(BF16)
    kn, kr, vt = _kv_proj(xs, kv_norm_g, wd_kv, ckv_norm_g, w_k, w_vt, cs, tile)
    for jl in range(n_b):
        l = n_a + jl
        wq = w_uq[jl].reshape(-1, heads, QK_NOPE + QK_ROPE)
        wq = jnp.concatenate([wq, _swap_halves(wq[..., QK_NOPE:])], axis=-1)
        wqt = wq.reshape(-1, heads * 2 * LANES).T.astype(BF16)
        qt = _q_proj(xs, norm_g[l, 0], w_dq[jl].astype(BF16), cq_norm_g[jl], wqt, cs.T, heads, tile)
        a = _attention(qt, kn, kr, vt, nb, seq, heads)
        xs = _oproj_residual(a, xs, w_o[jl].astype(BF16), norm_g[l, 1])
        xs = ffn(xs, l)
    return xs.reshape(nb, seq, d).astype(x.dtype)
```

```python
import functools
import math

import jax
import jax.numpy as jnp
from jax import lax
from jax.experimental import pallas as pl
from jax.experimental.pallas import tpu as pltpu

F32 = jnp.float32
BF16 = jnp.bfloat16

EPS = 1e-6
CHUNK = 64
QK_NOPE = 128
QK_ROPE = 64
V_HEAD = 128
ROPE_THETA = 10000.0
ATTN_SCALE = (QK_NOPE + QK_ROPE) ** -0.5
SSM_L = 16
S5_GROUP_BLOCK = 8
S5_ROW_PAD = 8
LANES = 128
NEG_INF = -1e30
V7X_VMEM_BYTES = 64 * 1024 * 1024


def _cparams(semantics, vmem_mib):
    assert vmem_mib * 1024 * 1024 < V7X_VMEM_BYTES
    return pltpu.CompilerParams(dimension_semantics=semantics,
                                vmem_limit_bytes=vmem_mib * 1024 * 1024)


def _tile(n, pref):
    t = min(n, pref)
    assert n % t == 0, (n, pref)
    return t


def _rms(x, g):
    return x * lax.rsqrt(jnp.mean(x * x, axis=-1, keepdims=True) + EPS) * g


def _gelu(x):
    c = math.sqrt(2.0 / math.pi)
    return 0.5 * x * (1.0 + jnp.tanh(c * (x + 0.044715 * (x * x * x))))


def _dot(a, b):
    return jnp.dot(a, b, preferred_element_type=F32)


def _resident(shape):
    nd = len(shape)
    return pl.BlockSpec(shape, lambda *_: (0,) * nd, pipeline_mode=pl.Buffered(1))


def _norm_kernel(x_ref, g_ref, o_ref):
    o_ref[...] = _rms(x_ref[...], g_ref[...]).astype(o_ref.dtype)


def _norm_cast(x, g, dtype):
    t, d = x.shape
    tm = _tile(t, 1024)
    return pl.pallas_call(
        _norm_kernel,
        grid=(t // tm,),
        in_specs=[pl.BlockSpec((tm, d), lambda i: (i, 0)), _resident((1, d))],
        out_specs=pl.BlockSpec((tm, d), lambda i: (i, 0)),
        out_shape=jax.ShapeDtypeStruct((t, d), dtype),
        compiler_params=_cparams(("parallel",), 40),
        name="norm_cast",
    )(x, g.reshape(1, d))


def _s5_weights(a_re, a_im, log_dt, b_re, b_im, c_re, c_im, d_skip):
    hi = lax.Precision.HIGHEST
    g, n = a_re.shape
    p = b_re.shape[-1]
    el = SSM_L
    dt = jnp.exp(log_dt)[:, None]
    mag = jnp.exp(dt * a_re)
    ab_re = mag * jnp.cos(dt * a_im)
    ab_im = mag * jnp.sin(dt * a_im)
    den = a_re * a_re + a_im * a_im
    nr, ni = ab_re - 1.0, ab_im
    coef_re = ((nr * a_re + ni * a_im) / den)[..., None]
    coef_im = ((ni * a_re - nr * a_im) / den)[..., None]
    bb_re = coef_re * b_re - coef_im * b_im
    bb_im = coef_re * b_im + coef_im * b_re
    prs, pis = [jnp.ones_like(ab_re)], [jnp.zeros_like(ab_re)]
    for _ in range(el):
        r, i = prs[-1], pis[-1]
        prs.append(r * ab_re - i * ab_im)
        pis.append(r * ab_im + i * ab_re)
    pr, pi = jnp.stack(prs), jnp.stack(pis)
    mr = c_re[None] * pr[:, :, None, :] - c_im[None] * pi[:, :, None, :]
    mi = c_re[None] * pi[:, :, None, :] + c_im[None] * pr[:, :, None, :]
    cb = (jnp.einsum("dgpn,gnq->gdpq", mr[:el], bb_re, precision=hi)
          - jnp.einsum("dgpn,gnq->gdpq", mi[:el], bb_im, precision=hi))
    lag = jnp.arange(el)[None, :] - jnp.arange(el)[:, None]
    t5 = cb[:, jnp.clip(lag, 0, el - 1)]
    t5 = jnp.where((lag >= 0)[None, :, :, None, None], t5, 0.0)
    eye_l = jnp.eye(el, dtype=F32)[None, :, :, None, None]
    eye_p = jnp.eye(p, dtype=F32)[None, None, None, :, :]
    t5 = t5 + eye_l * eye_p * d_skip[:, None, None, :, None]
    wy = t5.transpose(0, 1, 4, 2, 3).reshape(g, el * p, el * p)
    prr, pir = pr[el - 1::-1][:el], pi[el - 1::-1][:el]
    q_re = prr[:, :, :, None] * bb_re[None] - pir[:, :, :, None] * bb_im[None]
    q_im = prr[:, :, :, None] * bb_im[None] + pir[:, :, :, None] * bb_re[None]
    q_re = q_re.transpose(1, 0, 3, 2).reshape(g, el * p, n)
    q_im = q_im.transpose(1, 0, 3, 2).reshape(g, el * p, n)
    wz = jnp.concatenate([q_re, q_im, q_im, q_re], axis=-1)
    r_re = mr[1:].transpose(1, 3, 0, 2).reshape(g, n, el * p)
    r_im = -mi[1:].transpose(1, 3, 0, 2).reshape(g, n, el * p)
    wr = jnp.concatenate([r_re, r_im], axis=1)
    al_re, al_im = pr[el], pi[el]
    ac = jnp.stack([jnp.concatenate([al_re, al_re], -1),
                    jnp.concatenate([-al_im, al_im], -1),
                    jnp.concatenate([al_im, -al_im], -1)])
    return wz.astype(BF16), wy.astype(BF16), wr.astype(BF16), ac


S5_ROW_SUB = 64


def _s5_kernel(u_ref, wz_ref, wy_ref, wr_ref, ac_ref, o_ref, x_scr, y_scr, z_scr, zs_scr, h_scr, *,
               gb, rt, ns):
    stride = rt + S5_ROW_PAD
    p = LANES // gb
    for r0 in range(0, rt, S5_ROW_SUB):
        xs = [u_ref[pl.ds(r0 * SSM_L + s, S5_ROW_SUB, stride=SSM_L), :] for s in range(SSM_L)]
        for gi in range(gb):
            ug = jnp.concatenate([x[:, gi * p:(gi + 1) * p] for x in xs], axis=1)
            x_scr[gi, r0:r0 + S5_ROW_SUB, :] = ug.astype(BF16)
    for gi in range(gb):
        z = _dot(x_scr[gi], wz_ref[gi])
        z_scr[gi * stride:gi * stride + rt, :] = z[:, :ns]
        zs_scr[gi * stride:gi * stride + rt, :] = z[:, ns:]
    ar, ai, ais = ac_ref[0], ac_ref[1], ac_ref[2]

    @pl.when(pl.program_id(2) == 0)
    def _():
        h_scr[...] = jnp.zeros_like(h_scr)

    def body(c, carry):
        h, hs = carry
        rows = pl.ds(c, gb, stride=stride)
        zc = z_scr[rows, :]
        zsc = zs_scr[rows, :]
        z_scr[rows, :] = h
        return zc + ar * h + ai * hs, zsc + ar * hs + ais * h

    h, hs = lax.fori_loop(0, rt, body, (h_scr[0], h_scr[1]), unroll=8)
    h_scr[0] = h
    h_scr[1] = hs
    for gi in range(gb):
        hin = z_scr[gi * stride:gi * stride + rt, :].astype(BF16)
        y_scr[gi] = _gelu(_dot(x_scr[gi], wy_ref[gi]) + _dot(hin, wr_ref[gi]))
    for r0 in range(0, rt, S5_ROW_SUB):
        ys = [y_scr[gi, r0:r0 + S5_ROW_SUB, :] for gi in range(gb)]
        for s in range(SSM_L):
            row = jnp.concatenate([y[:, s * p:(s + 1) * p] for y in ys], axis=1)
            o_ref[pl.ds(r0 * SSM_L + s, S5_ROW_SUB, stride=SSM_L), :] = row


def _s5_scan(u, wz, wy, wr, ac, nb):
    t, d = u.shape
    g, lp, _ = wy.shape
    ns = wr.shape[1]
    gb = S5_GROUP_BLOCK
    assert gb * (lp // SSM_L) == LANES and g * (lp // SSM_L) == d
    nc = t // nb // SSM_L
    rt = _tile(nc, 256)
    assert rt % S5_ROW_SUB == 0
    nh = nc // rt
    kern = functools.partial(_s5_kernel, gb=gb, rt=rt, ns=ns)
    return pl.pallas_call(
        kern,
        grid=(g // gb, nb, nh),
        in_specs=[pl.BlockSpec((rt * SSM_L, LANES), lambda i, b, r: (b * nh + r, i)),
                  pl.BlockSpec((gb, lp, 2 * ns), lambda i, b, r: (i, 0, 0)),
                  pl.BlockSpec((gb, lp, lp), lambda i, b, r: (i, 0, 0)),
                  pl.BlockSpec((gb, ns, lp), lambda i, b, r: (i, 0, 0)),
                  pl.BlockSpec((3, gb, ns), lambda i, b, r: (0, i, 0))],
        out_specs=pl.BlockSpec((rt * SSM_L, LANES), lambda i, b, r: (b * nh + r, i)),
        out_shape=jax.ShapeDtypeStruct((t, d), F32),
        scratch_shapes=[pltpu.VMEM((gb, rt, lp), BF16),
                        pltpu.VMEM((gb, rt, lp), F32),
                        pltpu.VMEM((gb * (rt + S5_ROW_PAD), ns), F32),
                        pltpu.VMEM((gb * (rt + S5_ROW_PAD), ns), F32),
                        pltpu.VMEM((2, gb, ns), F32)],
        compiler_params=_cparams(("parallel", "arbitrary", "arbitrary"), 48),
        name="s5_scan",
    )(u, wz, wy, wr, ac)


def _glu_kernel(y_ref, x_ref, w_ref, b_ref, g_ref, o_ref):
    d = x_ref.shape[1]
    y = y_ref[...].astype(BF16)
    val = _dot(y, w_ref[:, :d]) + b_ref[:, :d]
    gate = _dot(y, w_ref[:, d:]) + b_ref[:, d:]
    m = val * jax.nn.sigmoid(gate)
    o_ref[...] = x_ref[...] + _rms(m, g_ref[...])


def _glu_residual(y, x, w, b, g):
    t, d = x.shape
    tm = _tile(t, 256)
    return pl.pallas_call(
        _glu_kernel,
        grid=(t // tm,),
        in_specs=[pl.BlockSpec((tm, d), lambda i: (i, 0)),
                  pl.BlockSpec((tm, d), lambda i: (i, 0)),
                  _resident((d, 2 * d)), _resident((1, 2 * d)), _resident((1, d))],
        out_specs=pl.BlockSpec((tm, d), lambda i: (i, 0)),
        out_shape=jax.ShapeDtypeStruct((t, d), F32),
        compiler_params=_cparams(("parallel",), 48),
        name="glu_residual",
    )(y, x, w, b.reshape(1, 2 * d), g.reshape(1, d))


FFN_HALO = 8


def _ffn_kernel(x_ref, xp_ref, g2_ref, wg_ref, wu_ref, cw_ref, cb_ref, wd_ref, g3_ref, o_ref,
                h_scr, acc_scr, *, tm, tiles_per_seq):
    i = pl.program_id(0)
    j = pl.program_id(1)

    @pl.when(j == 0)
    def _():
        h_scr[FFN_HALO:, :] = _rms(x_ref[...], g2_ref[...]).astype(BF16)
        keep = jnp.where(i % tiles_per_seq != 0, 1.0, 0.0)
        h_scr[:FFN_HALO, :] = (_rms(xp_ref[...], g2_ref[...]) * keep).astype(BF16)
        acc_scr[...] = jnp.zeros_like(acc_scr)

    ge = _dot(h_scr[...], wg_ref[...])
    up = _dot(h_scr[FFN_HALO:, :], wu_ref[...])
    cw = cw_ref[...]
    conv = (cw[2:3, :] * ge[FFN_HALO:, :] + cw[1:2, :] * ge[FFN_HALO - 1:FFN_HALO - 1 + tm, :]
            + cw[0:1, :] * ge[FFN_HALO - 2:FFN_HALO - 2 + tm, :] + cb_ref[...])
    act = (_gelu(conv) * up).astype(BF16)
    acc_scr[...] += _dot(act, wd_ref[...])

    @pl.when(j == pl.num_programs(1) - 1)
    def _():
        o_ref[...] = x_ref[...] + _rms(acc_scr[...], g3_ref[...])


def _ffn_residual(x, g2, wg, wu, cw, cb, wd, g3, seq):
    t, d = x.shape
    f = wg.shape[1]
    tm = _tile(seq, 512)
    tf = _tile(f, 512)
    hb = tm // FFN_HALO
    kern = functools.partial(_ffn_kernel, tm=tm, tiles_per_seq=seq // tm)
    return pl.pallas_call(
        kern,
        grid=(t // tm, f // tf),
        in_specs=[pl.BlockSpec((tm, d), lambda i, j: (i, 0)),
                  pl.BlockSpec((FFN_HALO, d), lambda i, j: (jnp.maximum(i * hb - 1, 0), 0)),
                  _resident((1, d)),
                  pl.BlockSpec((d, tf), lambda i, j: (0, j)),
                  pl.BlockSpec((d, tf), lambda i, j: (0, j)),
                  pl.BlockSpec((3, tf), lambda i, j: (0, j)),
                  pl.BlockSpec((1, tf), lambda i, j: (0, j)),
                  pl.BlockSpec((tf, d), lambda i, j: (j, 0)),
                  _resident((1, d))],
        out_specs=pl.BlockSpec((tm, d), lambda i, j: (i, 0)),
        out_shape=jax.ShapeDtypeStruct((t, d), F32),
        scratch_shapes=[pltpu.VMEM((FFN_HALO + tm, d), BF16), pltpu.VMEM((tm, d), F32)],
        compiler_params=_cparams(("parallel", "arbitrary"), 56),
        name="ffn_residual",
    )(x, x, g2.reshape(1, d), wg, wu, cw, cb.reshape(1, f), wd, g3.reshape(1, d))


ATTN_TK = 512
Q_SCALE = ATTN_SCALE * math.log2(math.e)
_NT = (((1,), (1,)), ((), ()))


def _rope_pair(v, cs):
    p = v * cs
    return p + pltpu.roll(p, QK_ROPE, axis=1)


def _kv_kernel(x_ref, g_ref, wd_ref, cg_ref, wk_ref, wvt_ref, cs_ref, kn_ref, kr_ref, vt_ref, *, rank):
    h = _rms(x_ref[...], g_ref[...]).astype(BF16)
    ckr = _dot(h, wd_ref[...])
    c = _rms(ckr[:, :rank], cg_ref[...]).astype(BF16)
    kn_ref[...] = _dot(c, wk_ref[...]).astype(kn_ref.dtype)
    vt_ref[0] = lax.dot_general(wvt_ref[...], c, _NT, preferred_element_type=F32).astype(vt_ref.dtype)
    r = _rope_pair(ckr[:, rank:], cs_ref[...])
    lane = lax.broadcasted_iota(jnp.int32, r.shape, 1)
    kr_ref[...] = jnp.where(lane < QK_ROPE, r, 0.0).astype(kr_ref.dtype)


def _kv_proj(x, g, wd, cg, wk, wvt, cs, tm):
    t, d = x.shape
    rank = wk.shape[0]
    nk = wk.shape[1]
    nv = wvt.shape[0]
    kern = functools.partial(_kv_kernel, rank=rank)
    return pl.pallas_call(
        kern,
        grid=(t // tm,),
        in_specs=[pl.BlockSpec((tm, d), lambda i: (i, 0)),
                  _resident((1, d)), _resident((d, rank + LANES)), _resident((1, rank)),
                  _resident((rank, nk)), _resident((nv, rank)),
                  pl.BlockSpec((tm, LANES), lambda i: (i, 0))],
        out_specs=[pl.BlockSpec((tm, nk), lambda i: (i, 0)),
                   pl.BlockSpec((tm, LANES), lambda i: (i, 0)),
                   pl.BlockSpec((1, nv, tm), lambda i: (i, 0, 0))],
        out_shape=[jax.ShapeDtypeStruct((t, nk), BF16), jax.ShapeDtypeStruct((t, LANES), BF16),
                   jax.ShapeDtypeStruct((t // tm, nv, tm), BF16)],
        compiler_params=_cparams(("parallel",), 48),
        name="kv_proj",
    )(x, g.reshape(1, d), wd, cg.reshape(1, rank), wk, wvt, cs)


def _q_kernel(x_ref, g_ref, wd_ref, cg_ref, wut_ref, cst_ref, q_ref, *, heads):
    h = _rms(x_ref[...], g_ref[...]).astype(BF16)
    cq = _rms(_dot(h, wd_ref[...]), cg_ref[...]).astype(BF16)
    cst = cst_ref[...]
    w = 2 * LANES
    for hd in range(heads):
        qh = lax.dot_general(wut_ref[hd * w:(hd + 1) * w, :], cq, _NT, preferred_element_type=F32)
        q_ref[0, hd * w:hd * w + LANES, :] = (qh[:LANES] * Q_SCALE).astype(q_ref.dtype)
        p = qh[LANES:] * cst
        r = ((p[:QK_ROPE] + p[QK_ROPE:]) * Q_SCALE).astype(q_ref.dtype)
        q_ref[0, hd * w + LANES:(hd + 1) * w, :] = jnp.concatenate([r, r], axis=0)


def _q_proj(x, g, wd, cg, wut, cst, heads, tm, tq):
    t, d = x.shape
    rank = wd.shape[1]
    nq = wut.shape[0]
    per = tq // tm
    kern = functools.partial(_q_kernel, heads=heads)
    return pl.pallas_call(
        kern,
        grid=(t // tm,),
        in_specs=[pl.BlockSpec((tm, d), lambda i: (i, 0)),
                  _resident((1, d)), _resident((d, rank)), _resident((1, rank)),
                  _resident((nq, rank)),
                  pl.BlockSpec((LANES, tm), lambda i: (0, i))],
        out_specs=pl.BlockSpec((1, nq, tm), lambda i: (i // per, 0, i % per)),
        out_shape=jax.ShapeDtypeStruct((t // tq, nq, tq), BF16),
        compiler_params=_cparams(("parallel",), 48),
        name="q_proj",
    )(x, g.reshape(1, d), wd, cg.reshape(1, rank), wut, cst)


def _chunk_mask(shape):
    kc = lax.broadcasted_iota(jnp.int32, shape, 0) // CHUNK
    qc = lax.broadcasted_iota(jnp.int32, shape, 1) // CHUNK
    return kc <= qc


def _attn_kernel(qt_ref, kn_ref, kr_ref, vt_ref, o_ref, sa_scr, sb_scr, m_scr, l_scr, acc_scr, *, tq, tk):
    i = pl.program_id(2)
    m_scr[...] = jnp.full_like(m_scr, NEG_INF)
    l_scr[...] = jnp.zeros_like(l_scr)
    acc_scr[...] = jnp.zeros_like(acc_scr)

    def keys(j):
        rows = pl.ds(pl.multiple_of(j * tk, tk), tk)
        return jnp.concatenate([kn_ref[rows, :], kr_ref[rows, :]], axis=1)

    def scores(j, dst):
        s = _dot(keys(j), qt_ref[0])
        dst[...] = s
        return jnp.max(s, axis=0, keepdims=True)

    def consume(s, mcol, j, lanes):
        m_prev = m_scr[:, lanes]
        m_new = jnp.maximum(m_prev, mcol)
        alpha = jnp.exp2(m_prev - m_new)
        p = jnp.exp2(s - m_new)
        l_scr[:, lanes] = alpha * l_scr[:, lanes] + jnp.sum(p, axis=0, keepdims=True)
        acc_scr[:, lanes] = alpha * acc_scr[:, lanes] + _dot(vt_ref[j], p.astype(BF16))
        m_scr[:, lanes] = m_new

    every = slice(0, tq)

    def pair(pp, mc_a):
        j = 2 * pp
        mc_b = scores(j + 1, sb_scr)
        consume(sa_scr[...], mc_a, j, every)
        mc_next = scores(j + 2, sa_scr)
        consume(sb_scr[...], mc_b, j + 1, every)
        return mc_next

    lax.fori_loop(0, i, pair, scores(0, sa_scr))
    s = jnp.where(_chunk_mask((tk, tq)), sa_scr[...], NEG_INF)
    consume(s, jnp.max(s, axis=0, keepdims=True), 2 * i, every)
    s = _dot(keys(2 * i + 1), qt_ref[0, :, tk:])
    s = jnp.where(_chunk_mask((tk, tk)), s, NEG_INF)
    consume(s, jnp.max(s, axis=0, keepdims=True), 2 * i + 1, slice(tk, tq))
    o_ref[...] = (acc_scr[...] / l_scr[...]).T.astype(o_ref.dtype)


def _attention(qt, kn, kr, vt, nb, seq, heads):
    tq = qt.shape[2]
    tk = vt.shape[2]
    assert tq == 2 * tk and tk % CHUNK == 0 and seq % tq == 0
    nq = seq // tq
    t = nb * seq
    kern = functools.partial(_attn_kernel, tq=tq, tk=tk)
    return pl.pallas_call(
        kern,
        grid=(nb, heads, nq),
        in_specs=[pl.BlockSpec((1, 2 * LANES, tq), lambda b, h, i: (b * nq + i, h, 0)),
                  pl.BlockSpec((seq, QK_NOPE), lambda b, h, i: (b, h)),
                  pl.BlockSpec((seq, LANES), lambda b, h, i: (b, 0)),
                  pl.BlockSpec((seq // tk, V_HEAD, tk), lambda b, h, i: (b, h, 0))],
        out_specs=pl.BlockSpec((tq, V_HEAD), lambda b, h, i: (b * nq + i, h)),
        out_shape=jax.ShapeDtypeStruct((t, heads * V_HEAD), BF16),
        scratch_shapes=[pltpu.VMEM((tk, tq), F32), pltpu.VMEM((tk, tq), F32),
                        pltpu.VMEM((1, tq), F32), pltpu.VMEM((1, tq), F32),
                        pltpu.VMEM((V_HEAD, tq), F32)],
        compiler_params=_cparams(("parallel", "parallel", "arbitrary"), 48),
        name="attention",
    )(qt, kn, kr, vt)


def _oproj_kernel(a_ref, x_ref, w_ref, g_ref, o_ref):
    o_ref[...] = x_ref[...] + _rms(_dot(a_ref[...], w_ref[...]), g_ref[...])


def _oproj_residual(a, x, w, g):
    t, d = x.shape
    k = a.shape[1]
    tm = _tile(t, 512)
    return pl.pallas_call(
        _oproj_kernel,
        grid=(t // tm,),
        in_specs=[pl.BlockSpec((tm, k), lambda i: (i, 0)),
                  pl.BlockSpec((tm, d), lambda i: (i, 0)),
                  _resident((k, d)), _resident((1, d))],
        out_specs=pl.BlockSpec((tm, d), lambda i: (i, 0)),
        out_shape=jax.ShapeDtypeStruct((t, d), F32),
        compiler_params=_cparams(("parallel",), 48),
        name="oproj_residual",
    )(a, x, w, g.reshape(1, d))


def _swap_halves(w):
    half = w.shape[-1] // 2
    return jnp.concatenate([w[..., half:], w[..., :half]], axis=-1)


def kernel(x, positions, norm_g, ssm_a_re, ssm_a_im, ssm_log_dt, ssm_b_re, ssm_b_im, ssm_c_re, ssm_c_im, ssm_d, ssm_w_glu, ssm_b_glu, kv_norm_g, w_dkv, ckv_norm_g, w_ukv, w_dq, cq_norm_g, w_uq, w_o, ffn_w_gate, ffn_w_up, ffn_conv_w, ffn_conv_b, ffn_w_down):
    nb, seq, d = x.shape
    t = nb * seq
    n_a = ssm_a_re.shape[0]
    n_b = w_dq.shape[0]
    xs = x.reshape(t, d).astype(F32)

    def ffn(xs, l):
        return _ffn_residual(xs, norm_g[l, 2], ffn_w_gate[l].astype(BF16), ffn_w_up[l].astype(BF16),
                             ffn_conv_w[l], ffn_conv_b[l], ffn_w_down[l].astype(BF16), norm_g[l, 3], seq)

    for l in range(n_a):
        wz, wy, wr, ac = _s5_weights(ssm_a_re[l], ssm_a_im[l], ssm_log_dt[l], ssm_b_re[l], ssm_b_im[l],
                                     ssm_c_re[l], ssm_c_im[l], ssm_d[l])
        u = _norm_cast(xs, norm_g[l, 0], F32)
        y = _s5_scan(u, wz, wy, wr, ac, nb)
        xs = _glu_residual(y, xs, ssm_w_glu[l].astype(BF16), ssm_b_glu[l], norm_g[l, 1])
        xs = ffn(xs, l)

    heads = w_o.shape[1] // V_HEAD
    rank_kv = w_ukv.shape[0]
    tile = _tile(seq // 2, ATTN_TK)
    freqs = ROPE_THETA ** (-jnp.arange(0, QK_ROPE, 2, dtype=F32) / QK_ROPE)
    ang = positions.reshape(t).astype(F32)[:, None] * freqs
    cos, sin = jnp.cos(ang), jnp.sin(ang)
    cs = jnp.concatenate([cos, cos, -sin, sin], axis=-1)
    wd_kv = jnp.concatenate([w_dkv, _swap_halves(w_dkv[:, rank_kv:])], axis=1).astype(BF16)
    w_kv = w_ukv.reshape(rank_kv, heads, QK_NOPE + V_HEAD)
    w_k = w_kv[..., :QK_NOPE].reshape(rank_kv, heads * QK_NOPE).astype(BF16)
    w_vt = w_kv[..., QK_NOPE:].reshape(rank_kv, heads * V_HEAD).T.astype(BF16)
    kn, kr, vt = _kv_proj(xs, kv_norm_g, wd_kv, ckv_norm_g, w_k, w_vt, cs, tile)
    for jl in range(n_b):
        l = n_a + jl
        wq = w_uq[jl].reshape(-1, heads, QK_NOPE + QK_ROPE)
        wq = jnp.concatenate([wq, _swap_halves(wq[..., QK_NOPE:])], axis=-1)
        wqt = wq.reshape(-1, heads * 2 * LANES).T.astype(BF16)
        qt = _q_proj(xs, norm_g[l, 0], w_dq[jl].astype(BF16), cq_norm_g[jl], wqt, cs.T, heads, tile, 2 * tile)
        a = _attention(qt, kn, kr, vt, nb, seq, heads)
        xs = _oproj_residual(a, xs, w_o[jl].astype(BF16), norm_g[l, 1])
        xs = ffn(xs, l)
    return xs.reshape(nb, seq, d).astype(x.dtype)
```

```python
import functools
import math

import jax
import jax.numpy as jnp
from jax import lax
from jax.experimental import pallas as pl
from jax.experimental.pallas import tpu as pltpu

F32 = jnp.float32
BF16 = jnp.bfloat16

EPS = 1e-6
CHUNK = 64
QK_NOPE = 128
QK_ROPE = 64
V_HEAD = 128
ROPE_THETA = 10000.0
ATTN_SCALE = (QK_NOPE + QK_ROPE) ** -0.5
SSM_L = 16
S5_GROUP_BLOCK = 8
S5_ROW_PAD = 8
LANES = 128
NEG_INF = -1e30
V7X_VMEM_BYTES = 64 * 1024 * 1024


def _cparams(semantics, vmem_mib):
    assert vmem_mib * 1024 * 1024 < V7X_VMEM_BYTES
    return pltpu.CompilerParams(dimension_semantics=semantics,
                                vmem_limit_bytes=vmem_mib * 1024 * 1024)


def _tile(n, pref):
    t = min(n, pref)
    assert n % t == 0, (n, pref)
    return t


def _rms(x, g):
    return x * lax.rsqrt(jnp.mean(x * x, axis=-1, keepdims=True) + EPS) * g


def _gelu(x):
    c = math.sqrt(2.0 / math.pi)
    return 0.5 * x * (1.0 + jnp.tanh(c * (x + 0.044715 * (x * x * x))))


def _dot(a, b):
    return jnp.dot(a, b, preferred_element_type=F32)


def _resident(shape):
    nd = len(shape)
    return pl.BlockSpec(shape, lambda *_: (0,) * nd, pipeline_mode=pl.Buffered(1))


def _norm_kernel(x_ref, g_ref, o_ref):
    o_ref[...] = _rms(x_ref[...], g_ref[...]).astype(o_ref.dtype)


def _norm_cast(x, g, dtype):
    t, d = x.shape
    tm = _tile(t, 1024)
    return pl.pallas_call(
        _norm_kernel,
        grid=(t // tm,),
        in_specs=[pl.BlockSpec((tm, d), lambda i: (i, 0)), _resident((1, d))],
        out_specs=pl.BlockSpec((tm, d), lambda i: (i, 0)),
        out_shape=jax.ShapeDtypeStruct((t, d), dtype),
        compiler_params=_cparams(("parallel",), 40),
        name="norm_cast",
    )(x, g.reshape(1, d))


def _s5_weights(a_re, a_im, log_dt, b_re, b_im, c_re, c_im, d_skip):
    hi = lax.Precision.HIGHEST
    g, n = a_re.shape
    p = b_re.shape[-1]
    el = SSM_L
    dt = jnp.exp(log_dt)[:, None]
    mag = jnp.exp(dt * a_re)
    ab_re = mag * jnp.cos(dt * a_im)
    ab_im = mag * jnp.sin(dt * a_im)
    den = a_re * a_re + a_im * a_im
    nr, ni = ab_re - 1.0, ab_im
    coef_re = ((nr * a_re + ni * a_im) / den)[..., None]
    coef_im = ((ni * a_re - nr * a_im) / den)[..., None]
    bb_re = coef_re * b_re - coef_im * b_im
    bb_im = coef_re * b_im + coef_im * b_re
    prs, pis = [jnp.ones_like(ab_re)], [jnp.zeros_like(ab_re)]
    for _ in range(el):
        r, i = prs[-1], pis[-1]
        prs.append(r * ab_re - i * ab_im)
        pis.append(r * ab_im + i * ab_re)
    pr, pi = jnp.stack(prs), jnp.stack(pis)
    mr = c_re[None] * pr[:, :, None, :] - c_im[None] * pi[:, :, None, :]
    mi = c_re[None] * pi[:, :, None, :] + c_im[None] * pr[:, :, None, :]
    cb = (jnp.einsum("dgpn,gnq->gdpq", mr[:el], bb_re, precision=hi)
          - jnp.einsum("dgpn,gnq->gdpq", mi[:el], bb_im, precision=hi))
    lag = jnp.arange(el)[None, :] - jnp.arange(el)[:, None]
    t5 = cb[:, jnp.clip(lag, 0, el - 1)]
    t5 = jnp.where((lag >= 0)[None, :, :, None, None], t5, 0.0)
    eye_l = jnp.eye(el, dtype=F32)[None, :, :, None, None]
    eye_p = jnp.eye(p, dtype=F32)[None, None, None, :, :]
    t5 = t5 + eye_l * eye_p * d_skip[:, None, None, :, None]
    wy = t5.transpose(0, 1, 4, 2, 3).reshape(g, el * p, el * p)
    prr, pir = pr[el - 1::-1][:el], pi[el - 1::-1][:el]
    q_re = prr[:, :, :, None] * bb_re[None] - pir[:, :, :, None] * bb_im[None]
    q_im = prr[:, :, :, None] * bb_im[None] + pir[:, :, :, None] * bb_re[None]
    q_re = q_re.transpose(1, 0, 3, 2).reshape(g, el * p, n)
    q_im = q_im.transpose(1, 0, 3, 2).reshape(g, el * p, n)
    wz = jnp.concatenate([q_re, q_im, q_im, q_re], axis=-1)
    r_re = mr[1:].transpose(1, 3, 0, 2).reshape(g, n, el * p)
    r_im = -mi[1:].transpose(1, 3, 0, 2).reshape(g, n, el * p)
    wr = jnp.concatenate([r_re, r_im], axis=1)
    al_re, al_im = pr[el], pi[el]
    ac = jnp.stack([jnp.concatenate([al_re, al_re], -1),
                    jnp.concatenate([-al_im, al_im], -1),
                    jnp.concatenate([al_im, -al_im], -1)])
    return wz.astype(BF16), wy.astype(BF16), wr.astype(BF16), ac


def _s5_column_order(g, gb, p):
    lane = jnp.arange(SSM_L * p)
    v, b, q = lane // (gb * p), (lane // p) % gb, lane % p
    gl = (jnp.arange(g) % gb)[:, None]
    return (v * gb + (b - gl) % gb) * p + q


def _s5_permute(wz, wy, wr, gb):
    g, lp, _ = wy.shape
    order = _s5_column_order(g, gb, lp // SSM_L)
    wz = jnp.take_along_axis(wz, order[:, :, None], axis=1)
    wy = jnp.take_along_axis(jnp.take_along_axis(wy, order[:, :, None], axis=1), order[:, None, :], axis=2)
    wr = jnp.take_along_axis(wr, order[:, None, :], axis=2)
    return wz, wy, wr


S5_ROW_SUB = 64


def _s5_kernel(u_ref, wz_ref, wy_ref, wr_ref, ac_ref, o_ref, x_scr, y_scr, z_scr, zs_scr, h_scr, *,
               gb, rt, ns):
    stride = rt + S5_ROW_PAD
    p = LANES // gb
    blk = lax.broadcasted_iota(jnp.int32, (S5_ROW_SUB, LANES), 1) // p
    for r0 in range(0, rt, S5_ROW_SUB):
        xr = []
        for s in range(SSM_L):
            x = u_ref[pl.ds(r0 * SSM_L + s, S5_ROW_SUB, stride=SSM_L), :]
            xr.append(pltpu.roll(x, (s % gb) * p, axis=1) if s % gb else x)
        for gi in range(gb):
            halves = []
            for v in range(SSM_L // gb):
                out = xr[v * gb]
                for s in range(v * gb + 1, (v + 1) * gb):
                    out = jnp.where(blk == (gi + s) % gb, xr[s], out)
                halves.append(out)
            x_scr[gi, r0:r0 + S5_ROW_SUB, :] = jnp.concatenate(halves, axis=1).astype(BF16)
    for gi in range(gb):
        z = _dot(x_scr[gi], wz_ref[gi])
        z_scr[gi * stride:gi * stride + rt, :] = z[:, :ns]
        zs_scr[gi * stride:gi * stride + rt, :] = z[:, ns:]
    ar, ai, ais = ac_ref[0], ac_ref[1], ac_ref[2]

    @pl.when(pl.program_id(2) == 0)
    def _():
        h_scr[...] = jnp.zeros_like(h_scr)

    def body(c, carry):
        h, hs = carry
        rows = pl.ds(c, gb, stride=stride)
        zc = z_scr[rows, :]
        zsc = zs_scr[rows, :]
        z_scr[rows, :] = h
        return zc + ar * h + ai * hs, zsc + ar * hs + ais * h

    h, hs = lax.fori_loop(0, rt, body, (h_scr[0], h_scr[1]), unroll=8)
    h_scr[0] = h
    h_scr[1] = hs
    for gi in range(gb):
        hin = z_scr[gi * stride:gi * stride + rt, :].astype(BF16)
        y_scr[gi] = _gelu(_dot(x_scr[gi], wy_ref[gi]) + _dot(hin, wr_ref[gi]))
    for r0 in range(0, rt, S5_ROW_SUB):
        ys = [y_scr[gi, r0:r0 + S5_ROW_SUB, :] for gi in range(gb)]
        for s in range(SSM_L):
            v = s // gb
            row = ys[0][:, v * LANES:(v + 1) * LANES]
            for gi in range(1, gb):
                row = jnp.where(blk == (gi + s) % gb, ys[gi][:, v * LANES:(v + 1) * LANES], row)
            if s % gb:
                row = pltpu.roll(row, LANES - (s % gb) * p, axis=1)
            o_ref[pl.ds(r0 * SSM_L + s, S5_ROW_SUB, stride=SSM_L), :] = row


def _s5_scan(u, wz, wy, wr, ac, nb):
    t, d = u.shape
    g, lp, _ = wy.shape
    ns = wr.shape[1]
    gb = S5_GROUP_BLOCK
    assert gb * (lp // SSM_L) == LANES and g * (lp // SSM_L) == d
    nc = t // nb // SSM_L
    rt = _tile(nc, 256)
    assert rt % S5_ROW_SUB == 0
    nh = nc // rt
    kern = functools.partial(_s5_kernel, gb=gb, rt=rt, ns=ns)
    return pl.pallas_call(
        kern,
        grid=(g // gb, nb, nh),
        in_specs=[pl.BlockSpec((rt * SSM_L, LANES), lambda i, b, r: (b * nh + r, i)),
                  pl.BlockSpec((gb, lp, 2 * ns), lambda i, b, r: (i, 0, 0)),
                  pl.BlockSpec((gb, lp, lp), lambda i, b, r: (i, 0, 0)),
                  pl.BlockSpec((gb, ns, lp), lambda i, b, r: (i, 0, 0)),
                  pl.BlockSpec((3, gb, ns), lambda i, b, r: (0, i, 0))],
        out_specs=pl.BlockSpec((rt * SSM_L, LANES), lambda i, b, r: (b * nh + r, i)),
        out_shape=jax.ShapeDtypeStruct((t, d), F32),
        scratch_shapes=[pltpu.VMEM((gb, rt, lp), BF16),
                        pltpu.VMEM((gb, rt, lp), F32),
                        pltpu.VMEM((gb * (rt + S5_ROW_PAD), ns), F32),
                        pltpu.VMEM((gb * (rt + S5_ROW_PAD), ns), F32),
                        pltpu.VMEM((2, gb, ns), F32)],
        compiler_params=_cparams(("parallel", "arbitrary", "arbitrary"), 48),
        name="s5_scan",
    )(u, wz, wy, wr, ac)


def _glu_kernel(y_ref, x_ref, w_ref, b_ref, g_ref, o_ref):
    d = x_ref.shape[1]
    y = y_ref[...].astype(BF16)
    val = _dot(y, w_ref[:, :d]) + b_ref[:, :d]
    gate = _dot(y, w_ref[:, d:]) + b_ref[:, d:]
    m = val * jax.nn.sigmoid(gate)
    o_ref[...] = x_ref[...] + _rms(m, g_ref[...])


def _glu_residual(y, x, w, b, g):
    t, d = x.shape
    tm = _tile(t, 256)
    return pl.pallas_call(
        _glu_kernel,
        grid=(t // tm,),
        in_specs=[pl.BlockSpec((tm, d), lambda i: (i, 0)),
                  pl.BlockSpec((tm, d), lambda i: (i, 0)),
                  _resident((d, 2 * d)), _resident((1, 2 * d)), _resident((1, d))],
        out_specs=pl.BlockSpec((tm, d), lambda i: (i, 0)),
        out_shape=jax.ShapeDtypeStruct((t, d), F32),
        compiler_params=_cparams(("parallel",), 48),
        name="glu_residual",
    )(y, x, w, b.reshape(1, 2 * d), g.reshape(1, d))


FFN_HALO = 8


def _ffn_kernel(x_ref, xp_ref, g2_ref, wg_ref, wu_ref, cw_ref, cb_ref, wd_ref, g3_ref, o_ref,
                h_scr, acc_scr, *, tm, tiles_per_seq):
    i = pl.program_id(0)
    j = pl.program_id(1)

    @pl.when(j == 0)
    def _():
        h_scr[FFN_HALO:, :] = _rms(x_ref[...], g2_ref[...]).astype(BF16)
        keep = jnp.where(i % tiles_per_seq != 0, 1.0, 0.0)
        h_scr[:FFN_HALO, :] = (_rms(xp_ref[...], g2_ref[...]) * keep).astype(BF16)
        acc_scr[...] = jnp.zeros_like(acc_scr)

    ge = _dot(h_scr[...], wg_ref[...])
    up = _dot(h_scr[FFN_HALO:, :], wu_ref[...])
    cw = cw_ref[...]
    conv = (cw[2:3, :] * ge[FFN_HALO:, :] + cw[1:2, :] * ge[FFN_HALO - 1:FFN_HALO - 1 + tm, :]
            + cw[0:1, :] * ge[FFN_HALO - 2:FFN_HALO - 2 + tm, :] + cb_ref[...])
    act = (_gelu(conv) * up).astype(BF16)
    acc_scr[...] += _dot(act, wd_ref[...])

    @pl.when(j == pl.num_programs(1) - 1)
    def _():
        o_ref[...] = x_ref[...] + _rms(acc_scr[...], g3_ref[...])


def _ffn_residual(x, g2, wg, wu, cw, cb, wd, g3, seq):
    t, d = x.shape
    f = wg.shape[1]
    tm = _tile(seq, 512)
    tf = _tile(f, 512)
    hb = tm // FFN_HALO
    kern = functools.partial(_ffn_kernel, tm=tm, tiles_per_seq=seq // tm)
    return pl.pallas_call(
        kern,
        grid=(t // tm, f // tf),
        in_specs=[pl.BlockSpec((tm, d), lambda i, j: (i, 0)),
                  pl.BlockSpec((FFN_HALO, d), lambda i, j: (jnp.maximum(i * hb - 1, 0), 0)),
                  _resident((1, d)),
                  pl.BlockSpec((d, tf), lambda i, j: (0, j)),
                  pl.BlockSpec((d, tf), lambda i, j: (0, j)),
                  pl.BlockSpec((3, tf), lambda i, j: (0, j)),
                  pl.BlockSpec((1, tf), lambda i, j: (0, j)),
                  pl.BlockSpec((tf, d), lambda i, j: (j, 0)),
                  _resident((1, d))],
        out_specs=pl.BlockSpec((tm, d), lambda i, j: (i, 0)),
        out_shape=jax.ShapeDtypeStruct((t, d), F32),
        scratch_shapes=[pltpu.VMEM((FFN_HALO + tm, d), BF16), pltpu.VMEM((tm, d), F32)],
        compiler_params=_cparams(("parallel", "arbitrary"), 56),
        name="ffn_residual",
    )(x, x, g2.reshape(1, d), wg, wu, cw, cb.reshape(1, f), wd, g3.reshape(1, d))


ATTN_TK = 512
Q_SCALE = ATTN_SCALE * math.log2(math.e)
_NT = (((1,), (1,)), ((), ()))


def _rope_pair(v, cs):
    p = v * cs
    return p + pltpu.roll(p, QK_ROPE, axis=1)


def _kv_kernel(x_ref, g_ref, wd_ref, cg_ref, wk_ref, wvt_ref, cs_ref, kn_ref, kr_ref, vt_ref, *, rank):
    h = _rms(x_ref[...], g_ref[...]).astype(BF16)
    ckr = _dot(h, wd_ref[...])
    c = _rms(ckr[:, :rank], cg_ref[...]).astype(BF16)
    kn_ref[...] = _dot(c, wk_ref[...]).astype(kn_ref.dtype)
    vt_ref[0] = lax.dot_general(wvt_ref[...], c, _NT, preferred_element_type=F32).astype(vt_ref.dtype)
    r = _rope_pair(ckr[:, rank:], cs_ref[...])
    lane = lax.broadcasted_iota(jnp.int32, r.shape, 1)
    kr_ref[...] = jnp.where(lane < QK_ROPE, r, 0.0).astype(kr_ref.dtype)


def _kv_proj(x, g, wd, cg, wk, wvt, cs, tm):
    t, d = x.shape
    rank = wk.shape[0]
    nk = wk.shape[1]
    nv = wvt.shape[0]
    kern = functools.partial(_kv_kernel, rank=rank)
    return pl.pallas_call(
        kern,
        grid=(t // tm,),
        in_specs=[pl.BlockSpec((tm, d), lambda i: (i, 0)),
                  _resident((1, d)), _resident((d, rank + LANES)), _resident((1, rank)),
                  _resident((rank, nk)), _resident((nv, rank)),
                  pl.BlockSpec((tm, LANES), lambda i: (i, 0))],
        out_specs=[pl.BlockSpec((tm, nk), lambda i: (i, 0)),
                   pl.BlockSpec((tm, LANES), lambda i: (i, 0)),
                   pl.BlockSpec((1, nv, tm), lambda i: (i, 0, 0))],
        out_shape=[jax.ShapeDtypeStruct((t, nk), BF16), jax.ShapeDtypeStruct((t, LANES), BF16),
                   jax.ShapeDtypeStruct((t // tm, nv, tm), BF16)],
        compiler_params=_cparams(("parallel",), 48),
        name="kv_proj",
    )(x, g.reshape(1, d), wd, cg.reshape(1, rank), wk, wvt, cs)


def _q_kernel(x_ref, g_ref, wd_ref, cg_ref, wut_ref, cst_ref, q_ref, *, heads):
    h = _rms(x_ref[...], g_ref[...]).astype(BF16)
    cq = _rms(_dot(h, wd_ref[...]), cg_ref[...]).astype(BF16)
    cst = cst_ref[...]
    w = 2 * LANES
    for hd in range(heads):
        qh = lax.dot_general(wut_ref[hd * w:(hd + 1) * w, :], cq, _NT, preferred_element_type=F32)
        q_ref[0, hd * w:hd * w + LANES, :] = (qh[:LANES] * Q_SCALE).astype(q_ref.dtype)
        p = qh[LANES:] * cst
        r = ((p[:QK_ROPE] + p[QK_ROPE:]) * Q_SCALE).astype(q_ref.dtype)
        q_ref[0, hd * w + LANES:(hd + 1) * w, :] = jnp.concatenate([r, r], axis=0)


def _q_proj(x, g, wd, cg, wut, cst, heads, tm, tq):
    t, d = x.shape
    rank = wd.shape[1]
    nq = wut.shape[0]
    per = tq // tm
    kern = functools.partial(_q_kernel, heads=heads)
    return pl.pallas_call(
        kern,
        grid=(t // tm,),
        in_specs=[pl.BlockSpec((tm, d), lambda i: (i, 0)),
                  _resident((1, d)), _resident((d, rank)), _resident((1, rank)),
                  _resident((nq, rank)),
                  pl.BlockSpec((LANES, tm), lambda i: (0, i))],
        out_specs=pl.BlockSpec((1, nq, tm), lambda i: (i // per, 0, i % per)),
        out_shape=jax.ShapeDtypeStruct((t // tq, nq, tq), BF16),
        compiler_params=_cparams(("parallel",), 48),
        name="q_proj",
    )(x, g.reshape(1, d), wd, cg.reshape(1, rank), wut, cst)


def _chunk_mask(shape):
    kc = lax.broadcasted_iota(jnp.int32, shape, 0) // CHUNK
    qc = lax.broadcasted_iota(jnp.int32, shape, 1) // CHUNK
    return kc <= qc


SUM_ROWS = 16


def _attn_kernel(qt_ref, kn_ref, kr_ref, vt_ref, o_ref, sa_scr, sb_scr, m_scr, acc_scr, *, tq, tk):
    i = pl.program_id(2)
    m_scr[...] = jnp.full_like(m_scr, NEG_INF)
    acc_scr[...] = jnp.zeros_like(acc_scr)
    ones = jnp.ones((SUM_ROWS, tk), BF16)

    def keys(j):
        rows = pl.ds(pl.multiple_of(j * tk, tk), tk)
        return jnp.concatenate([kn_ref[rows, :], kr_ref[rows, :]], axis=1)

    def scores(j, dst):
        s = _dot(keys(j), qt_ref[0])
        dst[...] = s
        return jnp.max(s, axis=0, keepdims=True)

    def consume(s, mcol, j, lanes):
        m_prev = m_scr[:, lanes]
        m_new = jnp.maximum(m_prev, mcol)
        alpha = jnp.exp2(m_prev - m_new)
        p = jnp.exp2(s - m_new).astype(BF16)
        v1 = jnp.concatenate([vt_ref[j], ones], axis=0)
        acc_scr[:, lanes] = alpha * acc_scr[:, lanes] + _dot(v1, p)
        m_scr[:, lanes] = m_new

    every = slice(0, tq)

    def pair(pp, mc_a):
        j = 2 * pp
        mc_b = scores(j + 1, sb_scr)
        consume(sa_scr[...], mc_a, j, every)
        mc_next = scores(j + 2, sa_scr)
        consume(sb_scr[...], mc_b, j + 1, every)
        return mc_next

    lax.fori_loop(0, i, pair, scores(0, sa_scr))
    s2 = _dot(keys(2 * i + 1), qt_ref[0, :, tk:])
    s = jnp.where(_chunk_mask((tk, tq)), sa_scr[...], NEG_INF)
    consume(s, jnp.max(s, axis=0, keepdims=True), 2 * i, every)
    s2 = jnp.where(_chunk_mask((tk, tk)), s2, NEG_INF)
    consume(s2, jnp.max(s2, axis=0, keepdims=True), 2 * i + 1, slice(tk, tq))
    acc = acc_scr[...]
    o_ref[...] = (acc[:V_HEAD] / acc[V_HEAD:V_HEAD + 1]).T.astype(o_ref.dtype)


def _attention(qt, kn, kr, vt, nb, seq, heads):
    tq = qt.shape[2]
    tk = vt.shape[2]
    assert tq == 2 * tk and tk % CHUNK == 0 and seq % tq == 0
    nq = seq // tq
    t = nb * seq
    kern = functools.partial(_attn_kernel, tq=tq, tk=tk)
    return pl.pallas_call(
        kern,
        grid=(nb, heads, nq),
        in_specs=[pl.BlockSpec((1, 2 * LANES, tq), lambda b, h, i: (b * nq + i, h, 0)),
                  pl.BlockSpec((seq, QK_NOPE), lambda b, h, i: (b, h)),
                  pl.BlockSpec((seq, LANES), lambda b, h, i: (b, 0)),
                  pl.BlockSpec((seq // tk, V_HEAD, tk), lambda b, h, i: (b, h, 0))],
        out_specs=pl.BlockSpec((tq, V_HEAD), lambda b, h, i: (b * nq + i, h)),
        out_shape=jax.ShapeDtypeStruct((t, heads * V_HEAD), BF16),
        scratch_shapes=[pltpu.VMEM((tk, tq), F32), pltpu.VMEM((tk, tq), F32),
                        pltpu.VMEM((1, tq), F32),
                        pltpu.VMEM((V_HEAD + SUM_ROWS, tq), F32)],
        compiler_params=_cparams(("parallel", "parallel", "arbitrary"), 48),
        name="attention",
    )(qt, kn, kr, vt)


def _oproj_kernel(a_ref, x_ref, w_ref, g_ref, o_ref):
    o_ref[...] = x_ref[...] + _rms(_dot(a_ref[...], w_ref[...]), g_ref[...])


def _oproj_residual(a, x, w, g):
    t, d = x.shape
    k = a.shape[1]
    tm = _tile(t, 512)
    return pl.pallas_call(
        _oproj_kernel,
        grid=(t // tm,),
        in_specs=[pl.BlockSpec((tm, k), lambda i: (i, 0)),
                  pl.BlockSpec((tm, d), lambda i: (i, 0)),
                  _resident((k, d)), _resident((1, d))],
        out_specs=pl.BlockSpec((tm, d), lambda i: (i, 0)),
        out_shape=jax.ShapeDtypeStruct((t, d), F32),
        compiler_params=_cparams(("parallel",), 48),
        name="oproj_residual",
    )(a, x, w, g.reshape(1, d))


def _swap_halves(w):
    half = w.shape[-1] // 2
    return jnp.concatenate([w[..., half:], w[..., :half]], axis=-1)


def kernel(x, positions, norm_g, ssm_a_re, ssm_a_im, ssm_log_dt, ssm_b_re, ssm_b_im, ssm_c_re, ssm_c_im, ssm_d, ssm_w_glu, ssm_b_glu, kv_norm_g, w_dkv, ckv_norm_g, w_ukv, w_dq, cq_norm_g, w_uq, w_o, ffn_w_gate, ffn_w_up, ffn_conv_w, ffn_conv_b, ffn_w_down):
    nb, seq, d = x.shape
    t = nb * seq
    n_a = ssm_a_re.shape[0]
    n_b = w_dq.shape[0]
    xs = x.reshape(t, d).astype(F32)

    def ffn(xs, l):
        return _ffn_residual(xs, norm_g[l, 2], ffn_w_gate[l].astype(BF16), ffn_w_up[l].astype(BF16),
                             ffn_conv_w[l], ffn_conv_b[l], ffn_w_down[l].astype(BF16), norm_g[l, 3], seq)

    for l in range(n_a):
        wz, wy, wr, ac = _s5_weights(ssm_a_re[l], ssm_a_im[l], ssm_log_dt[l], ssm_b_re[l], ssm_b_im[l],
                                     ssm_c_re[l], ssm_c_im[l], ssm_d[l])
        u = _norm_cast(xs, norm_g[l, 0], F32)
        wz, wy, wr = _s5_permute(wz, wy, wr, S5_GROUP_BLOCK)
        y = _s5_scan(u, wz, wy, wr, ac, nb)
        xs = _glu_residual(y, xs, ssm_w_glu[l].astype(BF16), ssm_b_glu[l], norm_g[l, 1])
        xs = ffn(xs, l)

    heads = w_o.shape[1] // V_HEAD
    rank_kv = w_ukv.shape[0]
    tile = _tile(seq // 2, ATTN_TK)
    freqs = ROPE_THETA ** (-jnp.arange(0, QK_ROPE, 2, dtype=F32) / QK_ROPE)
    ang = positions.reshape(t).astype(F32)[:, None] * freqs
    cos, sin = jnp.cos(ang), jnp.sin(ang)
    cs = jnp.concatenate([cos, cos, -sin, sin], axis=-1)
    wd_kv = jnp.concatenate([w_dkv, _swap_halves(w_dkv[:, rank_kv:])], axis=1).astype(BF16)
    w_kv = w_ukv.reshape(rank_kv, heads, QK_NOPE + V_HEAD)
    w_k = w_kv[..., :QK_NOPE].reshape(rank_kv, heads * QK_NOPE).astype(BF16)
    w_vt = w_kv[..., QK_NOPE:].reshape(rank_kv, heads * V_HEAD).T.astype(BF16)
    kn, kr, vt = _kv_proj(xs, kv_norm_g, wd_kv, ckv_norm_g, w_k, w_vt, cs, tile)
    for jl in range(n_b):
        l = n_a + jl
        wq = w_uq[jl].reshape(-1, heads, QK_NOPE + QK_ROPE)
        wq = jnp.concatenate([wq, _swap_halves(wq[..., QK_NOPE:])], axis=-1)
        wqt = wq.reshape(-1, heads * 2 * LANES).T.astype(BF16)
        qt = _q_proj(xs, norm_g[l, 0], w_dq[jl].astype(BF16), cq_norm_g[jl], wqt, cs.T, heads, tile, 2 * tile)
        a = _attention(qt, kn, kr, vt, nb, seq, heads)
        xs = _oproj_residual(a, xs, w_o[jl].astype(BF16), norm_g[l, 1])
        xs = ffn(xs, l)
    return xs.reshape(nb, seq, d).astype(x.dtype)
```

```python
import functools
import math

import jax
import jax.numpy as jnp
from jax import lax
from jax.experimental import pallas as pl
from jax.experimental.pallas import tpu as pltpu

F32 = jnp.float32
BF16 = jnp.bfloat16

EPS = 1e-6
CHUNK = 64
QK_NOPE = 128
QK_ROPE = 64
V_HEAD = 128
ROPE_THETA = 10000.0
ATTN_SCALE = (QK_NOPE + QK_ROPE) ** -0.5
SSM_L = 16
S5_GROUP_BLOCK = 8
S5_ROW_PAD = 8
LANES = 128
NEG_INF = -1e30
V7X_VMEM_BYTES = 64 * 1024 * 1024


def _cparams(semantics, vmem_mib):
    assert vmem_mib * 1024 * 1024 < V7X_VMEM_BYTES
    return pltpu.CompilerParams(dimension_semantics=semantics,
                                vmem_limit_bytes=vmem_mib * 1024 * 1024)


def _tile(n, pref):
    t = min(n, pref)
    assert n % t == 0, (n, pref)
    return t


def _rms(x, g):
    return x * lax.rsqrt(jnp.mean(x * x, axis=-1, keepdims=True) + EPS) * g


def _gelu(x):
    c = math.sqrt(2.0 / math.pi)
    return 0.5 * x * (1.0 + jnp.tanh(c * (x + 0.044715 * (x * x * x))))


def _dot(a, b):
    return jnp.dot(a, b, preferred_element_type=F32)


def _resident(shape):
    nd = len(shape)
    return pl.BlockSpec(shape, lambda *_: (0,) * nd, pipeline_mode=pl.Buffered(1))


def _norm_kernel(x_ref, g_ref, o_ref):
    o_ref[...] = _rms(x_ref[...], g_ref[...]).astype(o_ref.dtype)


def _norm_cast(x, g, dtype):
    t, d = x.shape
    tm = _tile(t, 1024)
    return pl.pallas_call(
        _norm_kernel,
        grid=(t // tm,),
        in_specs=[pl.BlockSpec((tm, d), lambda i: (i, 0)), _resident((1, d))],
        out_specs=pl.BlockSpec((tm, d), lambda i: (i, 0)),
        out_shape=jax.ShapeDtypeStruct((t, d), dtype),
        compiler_params=_cparams(("parallel",), 40),
        name="norm_cast",
    )(x, g.reshape(1, d))


def _s5_weights(a_re, a_im, log_dt, b_re, b_im, c_re, c_im, d_skip):
    hi = lax.Precision.HIGHEST
    g, n = a_re.shape
    p = b_re.shape[-1]
    el = SSM_L
    dt = jnp.exp(log_dt)[:, None]
    mag = jnp.exp(dt * a_re)
    ab_re = mag * jnp.cos(dt * a_im)
    ab_im = mag * jnp.sin(dt * a_im)
    den = a_re * a_re + a_im * a_im
    nr, ni = ab_re - 1.0, ab_im
    coef_re = ((nr * a_re + ni * a_im) / den)[..., None]
    coef_im = ((ni * a_re - nr * a_im) / den)[..., None]
    bb_re = coef_re * b_re - coef_im * b_im
    bb_im = coef_re * b_im + coef_im * b_re
    prs, pis = [jnp.ones_like(ab_re)], [jnp.zeros_like(ab_re)]
    for _ in range(el):
        r, i = prs[-1], pis[-1]
        prs.append(r * ab_re - i * ab_im)
        pis.append(r * ab_im + i * ab_re)
    pr, pi = jnp.stack(prs), jnp.stack(pis)
    mr = c_re[None] * pr[:, :, None, :] - c_im[None] * pi[:, :, None, :]
    mi = c_re[None] * pi[:, :, None, :] + c_im[None] * pr[:, :, None, :]
    cb = (jnp.einsum("dgpn,gnq->gdpq", mr[:el], bb_re, precision=hi)
          - jnp.einsum("dgpn,gnq->gdpq", mi[:el], bb_im, precision=hi))
    lag = jnp.arange(el)[None, :] - jnp.arange(el)[:, None]
    causal = (lag[None] == jnp.arange(el)[:, None, None]).astype(F32)
    t5 = jnp.einsum("gdpq,dts->gtspq", cb, causal, precision=hi)
    eye_l = jnp.eye(el, dtype=F32)[None, :, :, None, None]
    eye_p = jnp.eye(p, dtype=F32)[None, None, None, :, :]
    t5 = t5 + eye_l * eye_p * d_skip[:, None, None, :, None]
    wy = t5.transpose(0, 1, 4, 2, 3).reshape(g, el * p, el * p)
    prr, pir = pr[el - 1::-1][:el], pi[el - 1::-1][:el]
    q_re = prr[:, :, :, None] * bb_re[None] - pir[:, :, :, None] * bb_im[None]
    q_im = prr[:, :, :, None] * bb_im[None] + pir[:, :, :, None] * bb_re[None]
    q_re = q_re.transpose(1, 0, 3, 2).reshape(g, el * p, n)
    q_im = q_im.transpose(1, 0, 3, 2).reshape(g, el * p, n)
    wz = jnp.concatenate([q_re, q_im, q_im, q_re], axis=-1)
    r_re = mr[1:].transpose(1, 3, 0, 2).reshape(g, n, el * p)
    r_im = -mi[1:].transpose(1, 3, 0, 2).reshape(g, n, el * p)
    wr = jnp.concatenate([r_re, r_im], axis=1)
    al_re, al_im = pr[el], pi[el]
    ac = jnp.stack([jnp.concatenate([al_re, al_re], -1),
                    jnp.concatenate([-al_im, al_im], -1),
                    jnp.concatenate([al_im, -al_im], -1)])
    return wz.astype(BF16), wy.astype(BF16), wr.astype(BF16), ac


def _s5_roll_blocks(w, axis, gb):
    g, shp = w.shape[0], w.shape
    p = shp[axis] // SSM_L
    w = w.reshape((g // gb, gb) + shp[1:axis] + (SSM_L // gb, gb, p) + shp[axis + 1:])
    parts = [jnp.roll(w[:, gl], gl, axis=axis + 1) for gl in range(gb)]
    return jnp.stack(parts, axis=1).reshape(shp)


def _s5_permute(wz, wy, wr, gb):
    wy = _s5_roll_blocks(_s5_roll_blocks(wy, 1, gb), 2, gb)
    return _s5_roll_blocks(wz, 1, gb), wy, _s5_roll_blocks(wr, 2, gb)


S5_ROW_SUB = 64


def _s5_kernel(u_ref, wz_ref, wy_ref, wr_ref, ac_ref, o_ref, x_scr, y_scr, z_scr, zs_scr, h_scr, *,
               gb, rt, ns):
    stride = rt + S5_ROW_PAD
    p = LANES // gb
    blk = lax.broadcasted_iota(jnp.int32, (S5_ROW_SUB, LANES), 1) // p
    for r0 in range(0, rt, S5_ROW_SUB):
        xr = []
        for s in range(SSM_L):
            x = u_ref[pl.ds(r0 * SSM_L + s, S5_ROW_SUB, stride=SSM_L), :]
            xr.append(pltpu.roll(x, (s % gb) * p, axis=1) if s % gb else x)
        for gi in range(gb):
            halves = []
            for v in range(SSM_L // gb):
                out = xr[v * gb]
                for s in range(v * gb + 1, (v + 1) * gb):
                    out = jnp.where(blk == (gi + s) % gb, xr[s], out)
                halves.append(out)
            x_scr[gi, r0:r0 + S5_ROW_SUB, :] = jnp.concatenate(halves, axis=1).astype(BF16)
    for gi in range(gb):
        z = _dot(x_scr[gi], wz_ref[gi])
        z_scr[gi * stride:gi * stride + rt, :] = z[:, :ns]
        zs_scr[gi * stride:gi * stride + rt, :] = z[:, ns:]
    ar, ai, ais = ac_ref[0], ac_ref[1], ac_ref[2]

    @pl.when(pl.program_id(2) == 0)
    def _():
        h_scr[...] = jnp.zeros_like(h_scr)

    def body(c, carry):
        h, hs = carry
        rows = pl.ds(c, gb, stride=stride)
        zc = z_scr[rows, :]
        zsc = zs_scr[rows, :]
        z_scr[rows, :] = h
        return zc + ar * h + ai * hs, zsc + ar * hs + ais * h

    h, hs = lax.fori_loop(0, rt, body, (h_scr[0], h_scr[1]), unroll=8)
    h_scr[0] = h
    h_scr[1] = hs
    for gi in range(gb):
        hin = z_scr[gi * stride:gi * stride + rt, :].astype(BF16)
        y_scr[gi] = _gelu(_dot(x_scr[gi], wy_ref[gi]) + _dot(hin, wr_ref[gi]))
    for r0 in range(0, rt, S5_ROW_SUB):
        ys = [y_scr[gi, r0:r0 + S5_ROW_SUB, :] for gi in range(gb)]
        for s in range(SSM_L):
            v = s // gb
            row = ys[0][:, v * LANES:(v + 1) * LANES]
            for gi in range(1, gb):
                row = jnp.where(blk == (gi + s) % gb, ys[gi][:, v * LANES:(v + 1) * LANES], row)
            if s % gb:
                row = pltpu.roll(row, LANES - (s % gb) * p, axis=1)
            o_ref[pl.ds(r0 * SSM_L + s, S5_ROW_SUB, stride=SSM_L), :] = row


def _s5_scan(u, wz, wy, wr, ac, nb):
    t, d = u.shape
    g, lp, _ = wy.shape
    ns = wr.shape[1]
    gb = S5_GROUP_BLOCK
    assert gb * (lp // SSM_L) == LANES and g * (lp // SSM_L) == d
    nc = t // nb // SSM_L
    rt = _tile(nc, 256)
    assert rt % S5_ROW_SUB == 0
    nh = nc // rt
    kern = functools.partial(_s5_kernel, gb=gb, rt=rt, ns=ns)
    return pl.pallas_call(
        kern,
        grid=(g // gb, nb, nh),
        in_specs=[pl.BlockSpec((rt * SSM_L, LANES), lambda i, b, r: (b * nh + r, i)),
                  pl.BlockSpec((gb, lp, 2 * ns), lambda i, b, r: (i, 0, 0)),
                  pl.BlockSpec((gb, lp, lp), lambda i, b, r: (i, 0, 0)),
                  pl.BlockSpec((gb, ns, lp), lambda i, b, r: (i, 0, 0)),
                  pl.BlockSpec((3, gb, ns), lambda i, b, r: (0, i, 0))],
        out_specs=pl.BlockSpec((rt * SSM_L, LANES), lambda i, b, r: (b * nh + r, i)),
        out_shape=jax.ShapeDtypeStruct((t, d), F32),
        scratch_shapes=[pltpu.VMEM((gb, rt, lp), BF16),
                        pltpu.VMEM((gb, rt, lp), F32),
                        pltpu.VMEM((gb * (rt + S5_ROW_PAD), ns), F32),
                        pltpu.VMEM((gb * (rt + S5_ROW_PAD), ns), F32),
                        pltpu.VMEM((2, gb, ns), F32)],
        compiler_params=_cparams(("parallel", "arbitrary", "arbitrary"), 48),
        name="s5_scan",
    )(u, wz, wy, wr, ac)


def _glu_kernel(y_ref, x_ref, w_ref, b_ref, g_ref, o_ref):
    d = x_ref.shape[1]
    y = y_ref[...].astype(BF16)
    val = _dot(y, w_ref[:, :d]) + b_ref[:, :d]
    gate = _dot(y, w_ref[:, d:]) + b_ref[:, d:]
    m = val * jax.nn.sigmoid(gate)
    o_ref[...] = x_ref[...] + _rms(m, g_ref[...])


def _glu_residual(y, x, w, b, g):
    t, d = x.shape
    tm = _tile(t, 256)
    return pl.pallas_call(
        _glu_kernel,
        grid=(t // tm,),
        in_specs=[pl.BlockSpec((tm, d), lambda i: (i, 0)),
                  pl.BlockSpec((tm, d), lambda i: (i, 0)),
                  _resident((d, 2 * d)), _resident((1, 2 * d)), _resident((1, d))],
        out_specs=pl.BlockSpec((tm, d), lambda i: (i, 0)),
        out_shape=jax.ShapeDtypeStruct((t, d), F32),
        compiler_params=_cparams(("parallel",), 48),
        name="glu_residual",
    )(y, x, w, b.reshape(1, 2 * d), g.reshape(1, d))


FFN_HALO = 8


def _ffn_kernel(x_ref, xp_ref, g2_ref, wg_ref, wu_ref, cw_ref, cb_ref, wd_ref, g3_ref, o_ref,
                h_scr, acc_scr, *, tm, tiles_per_seq):
    i = pl.program_id(0)
    j = pl.program_id(1)

    @pl.when(j == 0)
    def _():
        h_scr[FFN_HALO:, :] = _rms(x_ref[...], g2_ref[...]).astype(BF16)
        keep = jnp.where(i % tiles_per_seq != 0, 1.0, 0.0)
        h_scr[:FFN_HALO, :] = (_rms(xp_ref[...], g2_ref[...]) * keep).astype(BF16)
        acc_scr[...] = jnp.zeros_like(acc_scr)

    ge = _dot(h_scr[...], wg_ref[...])
    up = _dot(h_scr[FFN_HALO:, :], wu_ref[...])
    cw = cw_ref[...]
    conv = (cw[2:3, :] * ge[FFN_HALO:, :] + cw[1:2, :] * ge[FFN_HALO - 1:FFN_HALO - 1 + tm, :]
            + cw[0:1, :] * ge[FFN_HALO - 2:FFN_HALO - 2 + tm, :] + cb_ref[...])
    act = (_gelu(conv) * up).astype(BF16)
    acc_scr[...] += _dot(act, wd_ref[...])

    @pl.when(j == pl.num_programs(1) - 1)
    def _():
        o_ref[...] = x_ref[...] + _rms(acc_scr[...], g3_ref[...])


def _ffn_residual(x, g2, wg, wu, cw, cb, wd, g3, seq):
    t, d = x.shape
    f = wg.shape[1]
    tm = _tile(seq, 512)
    tf = _tile(f, 512)
    hb = tm // FFN_HALO
    kern = functools.partial(_ffn_kernel, tm=tm, tiles_per_seq=seq // tm)
    return pl.pallas_call(
        kern,
        grid=(t // tm, f // tf),
        in_specs=[pl.BlockSpec((tm, d), lambda i, j: (i, 0)),
                  pl.BlockSpec((FFN_HALO, d), lambda i, j: (jnp.maximum(i * hb - 1, 0), 0)),
                  _resident((1, d)),
                  pl.BlockSpec((d, tf), lambda i, j: (0, j)),
                  pl.BlockSpec((d, tf), lambda i, j: (0, j)),
                  pl.BlockSpec((3, tf), lambda i, j: (0, j)),
                  pl.BlockSpec((1, tf), lambda i, j: (0, j)),
                  pl.BlockSpec((tf, d), lambda i, j: (j, 0)),
                  _resident((1, d))],
        out_specs=pl.BlockSpec((tm, d), lambda i, j: (i, 0)),
        out_shape=jax.ShapeDtypeStruct((t, d), F32),
        scratch_shapes=[pltpu.VMEM((FFN_HALO + tm, d), BF16), pltpu.VMEM((tm, d), F32)],
        compiler_params=_cparams(("parallel", "arbitrary"), 56),
        name="ffn_residual",
    )(x, x, g2.reshape(1, d), wg, wu, cw, cb.reshape(1, f), wd, g3.reshape(1, d))


ATTN_TK = 512
Q_SCALE = ATTN_SCALE * math.log2(math.e)
_NT = (((1,), (1,)), ((), ()))


def _rope_pair(v, cs):
    p = v * cs
    return p + pltpu.roll(p, QK_ROPE, axis=1)


def _kv_kernel(x_ref, g_ref, wd_ref, cg_ref, wk_ref, wvt_ref, cs_ref, kn_ref, kr_ref, vt_ref, *, rank):
    h = _rms(x_ref[...], g_ref[...]).astype(BF16)
    ckr = _dot(h, wd_ref[...])
    c = _rms(ckr[:, :rank], cg_ref[...]).astype(BF16)
    kn_ref[...] = _dot(c, wk_ref[...]).astype(kn_ref.dtype)
    vt_ref[0] = lax.dot_general(wvt_ref[...], c, _NT, preferred_element_type=F32).astype(vt_ref.dtype)
    r = _rope_pair(ckr[:, rank:], cs_ref[...])
    lane = lax.broadcasted_iota(jnp.int32, r.shape, 1)
    kr_ref[...] = jnp.where(lane < QK_ROPE, r, 0.0).astype(kr_ref.dtype)


def _kv_proj(x, g, wd, cg, wk, wvt, cs, tm):
    t, d = x.shape
    rank = wk.shape[0]
    nk = wk.shape[1]
    nv = wvt.shape[0]
    kern = functools.partial(_kv_kernel, rank=rank)
    return pl.pallas_call(
        kern,
        grid=(t // tm,),
        in_specs=[pl.BlockSpec((tm, d), lambda i: (i, 0)),
                  _resident((1, d)), _resident((d, rank + LANES)), _resident((1, rank)),
                  _resident((rank, nk)), _resident((nv, rank)),
                  pl.BlockSpec((tm, LANES), lambda i: (i, 0))],
        out_specs=[pl.BlockSpec((tm, nk), lambda i: (i, 0)),
                   pl.BlockSpec((tm, LANES), lambda i: (i, 0)),
                   pl.BlockSpec((1, nv, tm), lambda i: (i, 0, 0))],
        out_shape=[jax.ShapeDtypeStruct((t, nk), BF16), jax.ShapeDtypeStruct((t, LANES), BF16),
                   jax.ShapeDtypeStruct((t // tm, nv, tm), BF16)],
        compiler_params=_cparams(("parallel",), 48),
        name="kv_proj",
    )(x, g.reshape(1, d), wd, cg.reshape(1, rank), wk, wvt, cs)


def _q_kernel(x_ref, g_ref, wd_ref, cg_ref, wut_ref, cst_ref, q_ref, *, heads):
    h = _rms(x_ref[...], g_ref[...]).astype(BF16)
    cq = _rms(_dot(h, wd_ref[...]), cg_ref[...]).astype(BF16)
    cst = cst_ref[...]
    w = 2 * LANES
    for hd in range(heads):
        qh = lax.dot_general(wut_ref[hd * w:(hd + 1) * w, :], cq, _NT, preferred_element_type=F32)
        q_ref[0, hd * w:hd * w + LANES, :] = (qh[:LANES] * Q_SCALE).astype(q_ref.dtype)
        p = qh[LANES:] * cst
        r = ((p[:QK_ROPE] + p[QK_ROPE:]) * Q_SCALE).astype(q_ref.dtype)
        q_ref[0, hd * w + LANES:(hd + 1) * w, :] = jnp.concatenate([r, r], axis=0)


def _q_proj(x, g, wd, cg, wut, cst, heads, tm, tq):
    t, d = x.shape
    rank = wd.shape[1]
    nq = wut.shape[0]
    per = tq // tm
    kern = functools.partial(_q_kernel, heads=heads)
    return pl.pallas_call(
        kern,
        grid=(t // tm,),
        in_specs=[pl.BlockSpec((tm, d), lambda i: (i, 0)),
                  _resident((1, d)), _resident((d, rank)), _resident((1, rank)),
                  _resident((nq, rank)),
                  pl.BlockSpec((LANES, tm), lambda i: (0, i))],
        out_specs=pl.BlockSpec((1, nq, tm), lambda i: (i // per, 0, i % per)),
        out_shape=jax.ShapeDtypeStruct((t // tq, nq, tq), BF16),
        compiler_params=_cparams(("parallel",), 48),
        name="q_proj",
    )(x, g.reshape(1, d), wd, cg.reshape(1, rank), wut, cst)


def _chunk_mask(shape):
    kc = lax.broadcasted_iota(jnp.int32, shape, 0) // CHUNK
    qc = lax.broadcasted_iota(jnp.int32, shape, 1) // CHUNK
    return kc <= qc


SUM_ROWS = 16


def _attn_kernel(qt_ref, kn_ref, kr_ref, vt_ref, o_ref, sa_scr, sb_scr, m_scr, acc_scr, *, tq, tk):
    i = pl.program_id(2)
    m_scr[...] = jnp.full_like(m_scr, NEG_INF)
    acc_scr[...] = jnp.zeros_like(acc_scr)
    ones = jnp.ones((SUM_ROWS, tk), BF16)

    def keys(j):
        rows = pl.ds(pl.multiple_of(j * tk, tk), tk)
        return jnp.concatenate([kn_ref[rows, :], kr_ref[rows, :]], axis=1)

    def scores(j, dst):
        s = _dot(keys(j), qt_ref[0])
        dst[...] = s
        return jnp.max(s, axis=0, keepdims=True)

    def consume(s, mcol, j, lanes):
        m_prev = m_scr[:, lanes]
        m_new = jnp.maximum(m_prev, mcol)
        alpha = jnp.exp2(m_prev - m_new)
        p = jnp.exp2(s - m_new).astype(BF16)
        v1 = jnp.concatenate([vt_ref[j], ones], axis=0)
        acc_scr[:, lanes] = alpha * acc_scr[:, lanes] + _dot(v1, p)
        m_scr[:, lanes] = m_new

    every = slice(0, tq)

    def pair(pp, mc_a):
        j = 2 * pp
        mc_b = scores(j + 1, sb_scr)
        consume(sa_scr[...], mc_a, j, every)
        mc_next = scores(j + 2, sa_scr)
        consume(sb_scr[...], mc_b, j + 1, every)
        return mc_next

    lax.fori_loop(0, i, pair, scores(0, sa_scr))
    s2 = _dot(keys(2 * i + 1), qt_ref[0, :, tk:])
    s = jnp.where(_chunk_mask((tk, tq)), sa_scr[...], NEG_INF)
    consume(s, jnp.max(s, axis=0, keepdims=True), 2 * i, every)
    s2 = jnp.where(_chunk_mask((tk, tk)), s2, NEG_INF)
    consume(s2, jnp.max(s2, axis=0, keepdims=True), 2 * i + 1, slice(tk, tq))
    acc = acc_scr[...]
    o_ref[...] = (acc[:V_HEAD] / acc[V_HEAD:V_HEAD + 1]).T.astype(o_ref.dtype)


def _attention(qt, kn, kr, vt, nb, seq, heads):
    tq = qt.shape[2]
    tk = vt.shape[2]
    assert tq == 2 * tk and tk % CHUNK == 0 and seq % tq == 0
    nq = seq // tq
    t = nb * seq
    kern = functools.partial(_attn_kernel, tq=tq, tk=tk)
    return pl.pallas_call(
        kern,
        grid=(nb, heads, nq),
        in_specs=[pl.BlockSpec((1, 2 * LANES, tq), lambda b, h, i: (b * nq + i, h, 0)),
                  pl.BlockSpec((seq, QK_NOPE), lambda b, h, i: (b, h)),
                  pl.BlockSpec((seq, LANES), lambda b, h, i: (b, 0)),
                  pl.BlockSpec((seq // tk, V_HEAD, tk), lambda b, h, i: (b, h, 0))],
        out_specs=pl.BlockSpec((tq, V_HEAD), lambda b, h, i: (b * nq + i, h)),
        out_shape=jax.ShapeDtypeStruct((t, heads * V_HEAD), BF16),
        scratch_shapes=[pltpu.VMEM((tk, tq), F32), pltpu.VMEM((tk, tq), F32),
                        pltpu.VMEM((1, tq), F32),
                        pltpu.VMEM((V_HEAD + SUM_ROWS, tq), F32)],
        compiler_params=_cparams(("parallel", "parallel", "arbitrary"), 48),
        name="attention",
    )(qt, kn, kr, vt)


def _oproj_kernel(a_ref, x_ref, w_ref, g_ref, o_ref):
    o_ref[...] = x_ref[...] + _rms(_dot(a_ref[...], w_ref[...]), g_ref[...])


def _oproj_residual(a, x, w, g):
    t, d = x.shape
    k = a.shape[1]
    tm = _tile(t, 512)
    return pl.pallas_call(
        _oproj_kernel,
        grid=(t // tm,),
        in_specs=[pl.BlockSpec((tm, k), lambda i: (i, 0)),
                  pl.BlockSpec((tm, d), lambda i: (i, 0)),
                  _resident((k, d)), _resident((1, d))],
        out_specs=pl.BlockSpec((tm, d), lambda i: (i, 0)),
        out_shape=jax.ShapeDtypeStruct((t, d), F32),
        compiler_params=_cparams(("parallel",), 48),
        name="oproj_residual",
    )(a, x, w, g.reshape(1, d))


def _swap_halves(w):
    half = w.shape[-1] // 2
    return jnp.concatenate([w[..., half:], w[..., :half]], axis=-1)


def kernel(x, positions, norm_g, ssm_a_re, ssm_a_im, ssm_log_dt, ssm_b_re, ssm_b_im, ssm_c_re, ssm_c_im, ssm_d, ssm_w_glu, ssm_b_glu, kv_norm_g, w_dkv, ckv_norm_g, w_ukv, w_dq, cq_norm_g, w_uq, w_o, ffn_w_gate, ffn_w_up, ffn_conv_w, ffn_conv_b, ffn_w_down):
    nb, seq, d = x.shape
    t = nb * seq
    n_a = ssm_a_re.shape[0]
    n_b = w_dq.shape[0]
    xs = x.reshape(t, d).astype(F32)

    def ffn(xs, l):
        return _ffn_residual(xs, norm_g[l, 2], ffn_w_gate[l].astype(BF16), ffn_w_up[l].astype(BF16),
                             ffn_conv_w[l], ffn_conv_b[l], ffn_w_down[l].astype(BF16), norm_g[l, 3], seq)

    for l in range(n_a):
        wz, wy, wr, ac = _s5_weights(ssm_a_re[l], ssm_a_im[l], ssm_log_dt[l], ssm_b_re[l], ssm_b_im[l],
                                     ssm_c_re[l], ssm_c_im[l], ssm_d[l])
        u = _norm_cast(xs, norm_g[l, 0], F32)
        wz, wy, wr = _s5_permute(wz, wy, wr, S5_GROUP_BLOCK)
        y = _s5_scan(u, wz, wy, wr, ac, nb)
        xs = _glu_residual(y, xs, ssm_w_glu[l].astype(BF16), ssm_b_glu[l], norm_g[l, 1])
        xs = ffn(xs, l)

    heads = w_o.shape[1] // V_HEAD
    rank_kv = w_ukv.shape[0]
    tile = _tile(seq // 2, ATTN_TK)
    freqs = ROPE_THETA ** (-jnp.arange(0, QK_ROPE, 2, dtype=F32) / QK_ROPE)
    ang = positions.reshape(t).astype(F32)[:, None] * freqs
    cos, sin = jnp.cos(ang), jnp.sin(ang)
    cs = jnp.concatenate([cos, cos, -sin, sin], axis=-1)
    wd_kv = jnp.concatenate([w_dkv, _swap_halves(w_dkv[:, rank_kv:])], axis=1).astype(BF16)
    w_kv = w_ukv.reshape(rank_kv, heads, QK_NOPE + V_HEAD)
    w_k = w_kv[..., :QK_NOPE].reshape(rank_kv, heads * QK_NOPE).astype(BF16)
    w_vt = w_kv[..., QK_NOPE:].reshape(rank_kv, heads * V_HEAD).T.astype(BF16)
    kn, kr, vt = _kv_proj(xs, kv_norm_g, wd_kv, ckv_norm_g, w_k, w_vt, cs, tile)
    for jl in range(n_b):
        l = n_a + jl
        wq = w_uq[jl].reshape(-1, heads, QK_NOPE + QK_ROPE)
        wq = jnp.concatenate([wq, _swap_halves(wq[..., QK_NOPE:])], axis=-1)
        wqt = wq.reshape(-1, heads * 2 * LANES).T.astype(BF16)
        qt = _q_proj(xs, norm_g[l, 0], w_dq[jl].astype(BF16), cq_norm_g[jl], wqt, cs.T, heads, tile, 2 * tile)
        a = _attention(qt, kn, kr, vt, nb, seq, heads)
        xs = _oproj_residual(a, xs, w_o[jl].astype(BF16), norm_g[l, 1])
        xs = ffn(xs, l)
    return xs.reshape(nb, seq, d).astype(x.dtype)
```

```python
import functools
import math

import jax
import numpy as np
import jax.numpy as jnp
from jax import lax
from jax.experimental import pallas as pl
from jax.experimental.pallas import tpu as pltpu

F32 = jnp.float32
BF16 = jnp.bfloat16

EPS = 1e-6
CHUNK = 64
QK_NOPE = 128
QK_ROPE = 64
V_HEAD = 128
ROPE_THETA = 10000.0
ATTN_SCALE = (QK_NOPE + QK_ROPE) ** -0.5
SSM_L = 16
S5_GROUP_BLOCK = 8
S5_ROW_PAD = 8
LANES = 128
NEG_INF = -1e30
V7X_VMEM_BYTES = 64 * 1024 * 1024


def _cparams(semantics, vmem_mib):
    assert vmem_mib * 1024 * 1024 < V7X_VMEM_BYTES
    return pltpu.CompilerParams(dimension_semantics=semantics,
                                vmem_limit_bytes=vmem_mib * 1024 * 1024)


def _tile(n, pref):
    t = min(n, pref)
    assert n % t == 0, (n, pref)
    return t


def _rms(x, g):
    return x * lax.rsqrt(jnp.mean(x * x, axis=-1, keepdims=True) + EPS) * g


def _gelu(x):
    c = math.sqrt(2.0 / math.pi)
    return 0.5 * x * (1.0 + jnp.tanh(c * (x + 0.044715 * (x * x * x))))


def _dot(a, b):
    return jnp.dot(a, b, preferred_element_type=F32)


def _resident(shape):
    nd = len(shape)
    return pl.BlockSpec(shape, lambda *_: (0,) * nd, pipeline_mode=pl.Buffered(1))


def _norm_kernel(x_ref, g_ref, o_ref):
    o_ref[...] = _rms(x_ref[...], g_ref[...]).astype(o_ref.dtype)


def _norm_cast(x, g, dtype):
    t, d = x.shape
    tm = _tile(t, 1024)
    return pl.pallas_call(
        _norm_kernel,
        grid=(t // tm,),
        in_specs=[pl.BlockSpec((tm, d), lambda i: (i, 0)), _resident((1, d))],
        out_specs=pl.BlockSpec((tm, d), lambda i: (i, 0)),
        out_shape=jax.ShapeDtypeStruct((t, d), dtype),
        compiler_params=_cparams(("parallel",), 40),
        name="norm_cast",
    )(x, g.reshape(1, d))


def _s5_position_order(gb):
    t = np.arange(SSM_L)
    gl = np.arange(gb)[:, None]
    return (t // gb) * gb + (t % gb - gl) % gb


def _s5_weights(a_re, a_im, log_dt, b_re, b_im, c_re, c_im, d_skip, gb):
    hi = lax.Precision.HIGHEST
    g, n = a_re.shape
    p = b_re.shape[-1]
    el = SSM_L
    x = g // gb
    dt = jnp.exp(log_dt)[:, None]
    mag = jnp.exp(dt * a_re)
    ab_re = mag * jnp.cos(dt * a_im)
    ab_im = mag * jnp.sin(dt * a_im)
    den = a_re * a_re + a_im * a_im
    nr, ni = ab_re - 1.0, ab_im
    coef_re = ((nr * a_re + ni * a_im) / den)[..., None]
    coef_im = ((ni * a_re - nr * a_im) / den)[..., None]
    bb_re = coef_re * b_re - coef_im * b_im
    bb_im = coef_re * b_im + coef_im * b_re
    prs, pis = [jnp.ones_like(ab_re)], [jnp.zeros_like(ab_re)]
    for _ in range(el):
        r, i = prs[-1], pis[-1]
        prs.append(r * ab_re - i * ab_im)
        pis.append(r * ab_im + i * ab_re)
    pr, pi = jnp.stack(prs), jnp.stack(pis)
    order = _s5_position_order(gb)
    d_all = np.arange(el + 1)
    lag_sel = (order[:, None, None, :] - order[:, None, :, None] == d_all[None, :el, None, None])
    lag_sel = jnp.asarray(lag_sel, F32)
    rev_sel = jnp.asarray(el - 1 - order[:, :, None] == d_all[None, None, :], F32)
    fwd_sel = jnp.asarray(order[:, :, None] + 1 == d_all[None, None, :], F32)
    mr = c_re[None] * pr[:el, :, None, :] - c_im[None] * pi[:el, :, None, :]
    mi = c_re[None] * pi[:el, :, None, :] + c_im[None] * pr[:el, :, None, :]
    cb = (jnp.einsum("dgpn,gnq->gdpq", mr, bb_re, precision=hi)
          - jnp.einsum("dgpn,gnq->gdpq", mi, bb_im, precision=hi))
    wy = jnp.einsum("xgdpq,gdut->xguqtp", cb.reshape(x, gb, el, p, p), lag_sel, precision=hi)
    skip = jnp.eye(el, dtype=F32)[:, None, :, None] * jnp.eye(p, dtype=F32)[None, :, None, :]
    wy = wy + skip * d_skip.reshape(x, gb, 1, 1, 1, p)
    wy = wy.reshape(g, el * p, el * p)
    pr5, pi5 = pr.reshape(el + 1, x, gb, n), pi.reshape(el + 1, x, gb, n)
    prr = jnp.einsum("gtd,dxgn->xgtn", rev_sel, pr5, precision=hi)[:, :, :, None, :]
    pir = jnp.einsum("gtd,dxgn->xgtn", rev_sel, pi5, precision=hi)[:, :, :, None, :]
    bq_re = bb_re.transpose(0, 2, 1).reshape(x, gb, 1, p, n)
    bq_im = bb_im.transpose(0, 2, 1).reshape(x, gb, 1, p, n)
    q_re = prr * bq_re - pir * bq_im
    q_im = prr * bq_im + pir * bq_re
    wz = jnp.concatenate([q_re, q_im, q_im, q_re], axis=-1).reshape(g, el * p, 4 * n)
    prf = jnp.einsum("gtd,dxgn->xgnt", fwd_sel, pr5, precision=hi)[..., None]
    pif = jnp.einsum("gtd,dxgn->xgnt", fwd_sel, pi5, precision=hi)[..., None]
    cn_re = c_re.transpose(0, 2, 1).reshape(x, gb, n, 1, p)
    cn_im = c_im.transpose(0, 2, 1).reshape(x, gb, n, 1, p)
    r_re = cn_re * prf - cn_im * pif
    r_im = -(cn_re * pif + cn_im * prf)
    wr = jnp.concatenate([r_re, r_im], axis=2).reshape(g, 2 * n, el * p)
    al_re, al_im = pr[el], pi[el]
    ac = jnp.stack([jnp.concatenate([al_re, al_re], -1),
                    jnp.concatenate([-al_im, al_im], -1),
                    jnp.concatenate([al_im, -al_im], -1)])
    return wz.astype(BF16), wy.astype(BF16), wr.astype(BF16), ac


S5_ROW_SUB = 64


def _s5_kernel(u_ref, wz_ref, wy_ref, wr_ref, ac_ref, o_ref, x_scr, y_scr, z_scr, zs_scr, h_scr, *,
               gb, rt, ns):
    stride = rt + S5_ROW_PAD
    p = LANES // gb
    blk = lax.broadcasted_iota(jnp.int32, (S5_ROW_SUB, LANES), 1) // p
    for r0 in range(0, rt, S5_ROW_SUB):
        xr = []
        for s in range(SSM_L):
            x = u_ref[pl.ds(r0 * SSM_L + s, S5_ROW_SUB, stride=SSM_L), :]
            xr.append(pltpu.roll(x, (s % gb) * p, axis=1) if s % gb else x)
        for gi in range(gb):
            halves = []
            for v in range(SSM_L // gb):
                out = xr[v * gb]
                for s in range(v * gb + 1, (v + 1) * gb):
                    out = jnp.where(blk == (gi + s) % gb, xr[s], out)
                halves.append(out)
            x_scr[gi, r0:r0 + S5_ROW_SUB, :] = jnp.concatenate(halves, axis=1).astype(BF16)
    for gi in range(gb):
        z = _dot(x_scr[gi], wz_ref[gi])
        z_scr[gi * stride:gi * stride + rt, :] = z[:, :ns]
        zs_scr[gi * stride:gi * stride + rt, :] = z[:, ns:]
    ar, ai, ais = ac_ref[0], ac_ref[1], ac_ref[2]

    @pl.when(pl.program_id(2) == 0)
    def _():
        h_scr[...] = jnp.zeros_like(h_scr)

    def body(c, carry):
        h, hs = carry
        rows = pl.ds(c, gb, stride=stride)
        zc = z_scr[rows, :]
        zsc = zs_scr[rows, :]
        z_scr[rows, :] = h
        return zc + ar * h + ai * hs, zsc + ar * hs + ais * h

    h, hs = lax.fori_loop(0, rt, body, (h_scr[0], h_scr[1]), unroll=8)
    h_scr[0] = h
    h_scr[1] = hs
    for gi in range(gb):
        hin = z_scr[gi * stride:gi * stride + rt, :].astype(BF16)
        y_scr[gi] = _gelu(_dot(x_scr[gi], wy_ref[gi]) + _dot(hin, wr_ref[gi]))
    for r0 in range(0, rt, S5_ROW_SUB):
        ys = [y_scr[gi, r0:r0 + S5_ROW_SUB, :] for gi in range(gb)]
        for s in range(SSM_L):
            v = s // gb
            row = ys[0][:, v * LANES:(v + 1) * LANES]
            for gi in range(1, gb):
                row = jnp.where(blk == (gi + s) % gb, ys[gi][:, v * LANES:(v + 1) * LANES], row)
            if s % gb:
                row = pltpu.roll(row, LANES - (s % gb) * p, axis=1)
            o_ref[pl.ds(r0 * SSM_L + s, S5_ROW_SUB, stride=SSM_L), :] = row


def _s5_scan(u, wz, wy, wr, ac, nb):
    t, d = u.shape
    g, lp, _ = wy.shape
    ns = wr.shape[1]
    gb = S5_GROUP_BLOCK
    assert gb * (lp // SSM_L) == LANES and g * (lp // SSM_L) == d
    nc = t // nb // SSM_L
    rt = _tile(nc, 256)
    assert rt % S5_ROW_SUB == 0
    nh = nc // rt
    kern = functools.partial(_s5_kernel, gb=gb, rt=rt, ns=ns)
    return pl.pallas_call(
        kern,
        grid=(g // gb, nb, nh),
        in_specs=[pl.BlockSpec((rt * SSM_L, LANES), lambda i, b, r: (b * nh + r, i)),
                  pl.BlockSpec((gb, lp, 2 * ns), lambda i, b, r: (i, 0, 0)),
                  pl.BlockSpec((gb, lp, lp), lambda i, b, r: (i, 0, 0)),
                  pl.BlockSpec((gb, ns, lp), lambda i, b, r: (i, 0, 0)),
                  pl.BlockSpec((3, gb, ns), lambda i, b, r: (0, i, 0))],
        out_specs=pl.BlockSpec((rt * SSM_L, LANES), lambda i, b, r: (b * nh + r, i)),
        out_shape=jax.ShapeDtypeStruct((t, d), F32),
        scratch_shapes=[pltpu.VMEM((gb, rt, lp), BF16),
                        pltpu.VMEM((gb, rt, lp), F32),
                        pltpu.VMEM((gb * (rt + S5_ROW_PAD), ns), F32),
                        pltpu.VMEM((gb * (rt + S5_ROW_PAD), ns), F32),
                        pltpu.VMEM((2, gb, ns), F32)],
        compiler_params=_cparams(("parallel", "arbitrary", "arbitrary"), 48),
        name="s5_scan",
    )(u, wz, wy, wr, ac)


def _glu_kernel(y_ref, x_ref, w_ref, b_ref, g_ref, o_ref):
    d = x_ref.shape[1]
    y = y_ref[...].astype(BF16)
    val = _dot(y, w_ref[:, :d]) + b_ref[:, :d]
    gate = _dot(y, w_ref[:, d:]) + b_ref[:, d:]
    m = val * jax.nn.sigmoid(gate)
    o_ref[...] = x_ref[...] + _rms(m, g_ref[...])


def _glu_residual(y, x, w, b, g):
    t, d = x.shape
    tm = _tile(t, 256)
    return pl.pallas_call(
        _glu_kernel,
        grid=(t // tm,),
        in_specs=[pl.BlockSpec((tm, d), lambda i: (i, 0)),
                  pl.BlockSpec((tm, d), lambda i: (i, 0)),
                  _resident((d, 2 * d)), _resident((1, 2 * d)), _resident((1, d))],
        out_specs=pl.BlockSpec((tm, d), lambda i: (i, 0)),
        out_shape=jax.ShapeDtypeStruct((t, d), F32),
        compiler_params=_cparams(("parallel",), 48),
        name="glu_residual",
    )(y, x, w, b.reshape(1, 2 * d), g.reshape(1, d))


FFN_HALO = 8


def _ffn_kernel(x_ref, xp_ref, g2_ref, wg_ref, wu_ref, cw_ref, cb_ref, wd_ref, g3_ref, o_ref,
                h_scr, acc_scr, *, tm, tiles_per_seq):
    i = pl.program_id(0)
    j = pl.program_id(1)

    @pl.when(j == 0)
    def _():
        h_scr[FFN_HALO:, :] = _rms(x_ref[...], g2_ref[...]).astype(BF16)
        keep = jnp.where(i % tiles_per_seq != 0, 1.0, 0.0)
        h_scr[:FFN_HALO, :] = (_rms(xp_ref[...], g2_ref[...]) * keep).astype(BF16)
        acc_scr[...] = jnp.zeros_like(acc_scr)

    ge = _dot(h_scr[...], wg_ref[...])
    up = _dot(h_scr[FFN_HALO:, :], wu_ref[...])
    cw = cw_ref[...]
    conv = (cw[2:3, :] * ge[FFN_HALO:, :] + cw[1:2, :] * ge[FFN_HALO - 1:FFN_HALO - 1 + tm, :]
            + cw[0:1, :] * ge[FFN_HALO - 2:FFN_HALO - 2 + tm, :] + cb_ref[...])
    act = (_gelu(conv) * up).astype(BF16)
    acc_scr[...] += _dot(act, wd_ref[...])

    @pl.when(j == pl.num_programs(1) - 1)
    def _():
        o_ref[...] = x_ref[...] + _rms(acc_scr[...], g3_ref[...])


def _ffn_residual(x, g2, wg, wu, cw, cb, wd, g3, seq):
    t, d = x.shape
    f = wg.shape[1]
    tm = _tile(seq, 512)
    tf = _tile(f, 512)
    hb = tm // FFN_HALO
    kern = functools.partial(_ffn_kernel, tm=tm, tiles_per_seq=seq // tm)
    return pl.pallas_call(
        kern,
        grid=(t // tm, f // tf),
        in_specs=[pl.BlockSpec((tm, d), lambda i, j: (i, 0)),
                  pl.BlockSpec((FFN_HALO, d), lambda i, j: (jnp.maximum(i * hb - 1, 0), 0)),
                  _resident((1, d)),
                  pl.BlockSpec((d, tf), lambda i, j: (0, j)),
                  pl.BlockSpec((d, tf), lambda i, j: (0, j)),
                  pl.BlockSpec((3, tf), lambda i, j: (0, j)),
                  pl.BlockSpec((1, tf), lambda i, j: (0, j)),
                  pl.BlockSpec((tf, d), lambda i, j: (j, 0)),
                  _resident((1, d))],
        out_specs=pl.BlockSpec((tm, d), lambda i, j: (i, 0)),
        out_shape=jax.ShapeDtypeStruct((t, d), F32),
        scratch_shapes=[pltpu.VMEM((FFN_HALO + tm, d), BF16), pltpu.VMEM((tm, d), F32)],
        compiler_params=_cparams(("parallel", "arbitrary"), 56),
        name="ffn_residual",
    )(x, x, g2.reshape(1, d), wg, wu, cw, cb.reshape(1, f), wd, g3.reshape(1, d))


ATTN_TK = 512
Q_SCALE = ATTN_SCALE * math.log2(math.e)
_NT = (((1,), (1,)), ((), ()))


def _rope_pair(v, cs):
    p = v * cs
    return p + pltpu.roll(p, QK_ROPE, axis=1)


def _kv_kernel(x_ref, g_ref, wd_ref, cg_ref, wk_ref, wvt_ref, cs_ref, kn_ref, kr_ref, vt_ref, *, rank):
    h = _rms(x_ref[...], g_ref[...]).astype(BF16)
    ckr = _dot(h, wd_ref[...])
    c = _rms(ckr[:, :rank], cg_ref[...]).astype(BF16)
    kn_ref[...] = _dot(c, wk_ref[...]).astype(kn_ref.dtype)
    vt_ref[0] = lax.dot_general(wvt_ref[...], c, _NT, preferred_element_type=F32).astype(vt_ref.dtype)
    r = _rope_pair(ckr[:, rank:], cs_ref[...])
    lane = lax.broadcasted_iota(jnp.int32, r.shape, 1)
    kr_ref[...] = jnp.where(lane < QK_ROPE, r, 0.0).astype(kr_ref.dtype)


def _kv_proj(x, g, wd, cg, wk, wvt, cs, tm):
    t, d = x.shape
    rank = wk.shape[0]
    nk = wk.shape[1]
    nv = wvt.shape[0]
    kern = functools.partial(_kv_kernel, rank=rank)
    return pl.pallas_call(
        kern,
        grid=(t // tm,),
        in_specs=[pl.BlockSpec((tm, d), lambda i: (i, 0)),
                  _resident((1, d)), _resident((d, rank + LANES)), _resident((1, rank)),
                  _resident((rank, nk)), _resident((nv, rank)),
                  pl.BlockSpec((tm, LANES), lambda i: (i, 0))],
        out_specs=[pl.BlockSpec((tm, nk), lambda i: (i, 0)),
                   pl.BlockSpec((tm, LANES), lambda i: (i, 0)),
                   pl.BlockSpec((1, nv, tm), lambda i: (i, 0, 0))],
        out_shape=[jax.ShapeDtypeStruct((t, nk), BF16), jax.ShapeDtypeStruct((t, LANES), BF16),
                   jax.ShapeDtypeStruct((t // tm, nv, tm), BF16)],
        compiler_params=_cparams(("parallel",), 48),
        name="kv_proj",
    )(x, g.reshape(1, d), wd, cg.reshape(1, rank), wk, wvt, cs)


def _q_kernel(x_ref, g_ref, wd_ref, cg_ref, wut_ref, cst_ref, q_ref, *, heads):
    h = _rms(x_ref[...], g_ref[...]).astype(BF16)
    cq = _rms(_dot(h, wd_ref[...]), cg_ref[...]).astype(BF16)
    cst = cst_ref[...]
    w = 2 * LANES
    for hd in range(heads):
        qh = lax.dot_general(wut_ref[hd * w:(hd + 1) * w, :], cq, _NT, preferred_element_type=F32)
        q_ref[0, hd * w:hd * w + LANES, :] = (qh[:LANES] * Q_SCALE).astype(q_ref.dtype)
        p = qh[LANES:] * cst
        r = ((p[:QK_ROPE] + p[QK_ROPE:]) * Q_SCALE).astype(q_ref.dtype)
        q_ref[0, hd * w + LANES:(hd + 1) * w, :] = jnp.concatenate([r, r], axis=0)


def _q_proj(x, g, wd, cg, wut, cst, heads, tm, tq):
    t, d = x.shape
    rank = wd.shape[1]
    nq = wut.shape[0]
    per = tq // tm
    kern = functools.partial(_q_kernel, heads=heads)
    return pl.pallas_call(
        kern,
        grid=(t // tm,),
        in_specs=[pl.BlockSpec((tm, d), lambda i: (i, 0)),
                  _resident((1, d)), _resident((d, rank)), _resident((1, rank)),
                  _resident((nq, rank)),
                  pl.BlockSpec((LANES, tm), lambda i: (0, i))],
        out_specs=pl.BlockSpec((1, nq, tm), lambda i: (i // per, 0, i % per)),
        out_shape=jax.ShapeDtypeStruct((t // tq, nq, tq), BF16),
        compiler_params=_cparams(("parallel",), 48),
        name="q_proj",
    )(x, g.reshape(1, d), wd, cg.reshape(1, rank), wut, cst)


def _chunk_mask(shape):
    kc = lax.broadcasted_iota(jnp.int32, shape, 0) // CHUNK
    qc = lax.broadcasted_iota(jnp.int32, shape, 1) // CHUNK
    return kc <= qc


SUM_ROWS = 16


def _attn_kernel(qt_ref, kn_ref, kr_ref, vt_ref, o_ref, sa_scr, sb_scr, m_scr, acc_scr, *, tq, tk):
    i = pl.program_id(2)
    m_scr[...] = jnp.full_like(m_scr, NEG_INF)
    acc_scr[...] = jnp.zeros_like(acc_scr)
    ones = jnp.ones((SUM_ROWS, tk), BF16)

    def keys(j):
        rows = pl.ds(pl.multiple_of(j * tk, tk), tk)
        return jnp.concatenate([kn_ref[rows, :], kr_ref[rows, :]], axis=1)

    def scores(j, dst):
        s = _dot(keys(j), qt_ref[0])
        dst[...] = s
        return jnp.max(s, axis=0, keepdims=True)

    def consume(s, mcol, j, lanes):
        m_prev = m_scr[:, lanes]
        m_new = jnp.maximum(m_prev, mcol)
        alpha = jnp.exp2(m_prev - m_new)
        p = jnp.exp2(s - m_new).astype(BF16)
        v1 = jnp.concatenate([vt_ref[j], ones], axis=0)
        acc_scr[:, lanes] = alpha * acc_scr[:, lanes] + _dot(v1, p)
        m_scr[:, lanes] = m_new

    every = slice(0, tq)

    def pair(pp, mc_a):
        j = 2 * pp
        mc_b = scores(j + 1, sb_scr)
        consume(sa_scr[...], mc_a, j, every)
        mc_next = scores(j + 2, sa_scr)
        consume(sb_scr[...], mc_b, j + 1, every)
        return mc_next

    lax.fori_loop(0, i, pair, scores(0, sa_scr))
    s2 = _dot(keys(2 * i + 1), qt_ref[0, :, tk:])
    s = jnp.where(_chunk_mask((tk, tq)), sa_scr[...], NEG_INF)
    consume(s, jnp.max(s, axis=0, keepdims=True), 2 * i, every)
    s2 = jnp.where(_chunk_mask((tk, tk)), s2, NEG_INF)
    consume(s2, jnp.max(s2, axis=0, keepdims=True), 2 * i + 1, slice(tk, tq))
    acc = acc_scr[...]
    o_ref[...] = (acc[:V_HEAD] / acc[V_HEAD:V_HEAD + 1]).T.astype(o_ref.dtype)


def _attention(qt, kn, kr, vt, nb, seq, heads):
    tq = qt.shape[2]
    tk = vt.shape[2]
    assert tq == 2 * tk and tk % CHUNK == 0 and seq % tq == 0
    nq = seq // tq
    t = nb * seq
    kern = functools.partial(_attn_kernel, tq=tq, tk=tk)
    return pl.pallas_call(
        kern,
        grid=(nb, heads, nq),
        in_specs=[pl.BlockSpec((1, 2 * LANES, tq), lambda b, h, i: (b * nq + i, h, 0)),
                  pl.BlockSpec((seq, QK_NOPE), lambda b, h, i: (b, h)),
                  pl.BlockSpec((seq, LANES), lambda b, h, i: (b, 0)),
                  pl.BlockSpec((seq // tk, V_HEAD, tk), lambda b, h, i: (b, h, 0))],
        out_specs=pl.BlockSpec((tq, V_HEAD), lambda b, h, i: (b * nq + i, h)),
        out_shape=jax.ShapeDtypeStruct((t, heads * V_HEAD), BF16),
        scratch_shapes=[pltpu.VMEM((tk, tq), F32), pltpu.VMEM((tk, tq), F32),
                        pltpu.VMEM((1, tq), F32),
                        pltpu.VMEM((V_HEAD + SUM_ROWS, tq), F32)],
        compiler_params=_cparams(("parallel", "parallel", "arbitrary"), 48),
        name="attention",
    )(qt, kn, kr, vt)


def _oproj_kernel(a_ref, x_ref, w_ref, g_ref, o_ref):
    o_ref[...] = x_ref[...] + _rms(_dot(a_ref[...], w_ref[...]), g_ref[...])


def _oproj_residual(a, x, w, g):
    t, d = x.shape
    k = a.shape[1]
    tm = _tile(t, 512)
    return pl.pallas_call(
        _oproj_kernel,
        grid=(t // tm,),
        in_specs=[pl.BlockSpec((tm, k), lambda i: (i, 0)),
                  pl.BlockSpec((tm, d), lambda i: (i, 0)),
                  _resident((k, d)), _resident((1, d))],
        out_specs=pl.BlockSpec((tm, d), lambda i: (i, 0)),
        out_shape=jax.ShapeDtypeStruct((t, d), F32),
        compiler_params=_cparams(("parallel",), 48),
        name="oproj_residual",
    )(a, x, w, g.reshape(1, d))


def _swap_halves(w):
    half = w.shape[-1] // 2
    return jnp.concatenate([w[..., half:], w[..., :half]], axis=-1)


def kernel(x, positions, norm_g, ssm_a_re, ssm_a_im, ssm_log_dt, ssm_b_re, ssm_b_im, ssm_c_re, ssm_c_im, ssm_d, ssm_w_glu, ssm_b_glu, kv_norm_g, w_dkv, ckv_norm_g, w_ukv, w_dq, cq_norm_g, w_uq, w_o, ffn_w_gate, ffn_w_up, ffn_conv_w, ffn_conv_b, ffn_w_down):
    nb, seq, d = x.shape
    t = nb * seq
    n_a = ssm_a_re.shape[0]
    n_b = w_dq.shape[0]
    xs = x.reshape(t, d).astype(F32)

    def ffn(xs, l):
        return _ffn_residual(xs, norm_g[l, 2], ffn_w_gate[l].astype(BF16), ffn_w_up[l].astype(BF16),
                             ffn_conv_w[l], ffn_conv_b[l], ffn_w_down[l].astype(BF16), norm_g[l, 3], seq)

    for l in range(n_a):
        wz, wy, wr, ac = _s5_weights(ssm_a_re[l], ssm_a_im[l], ssm_log_dt[l], ssm_b_re[l], ssm_b_im[l],
                                     ssm_c_re[l], ssm_c_im[l], ssm_d[l], S5_GROUP_BLOCK)
        u = _norm_cast(xs, norm_g[l, 0], F32)
        y = _s5_scan(u, wz, wy, wr, ac, nb)
        xs = _glu_residual(y, xs, ssm_w_glu[l].astype(BF16), ssm_b_glu[l], norm_g[l, 1])
        xs = ffn(xs, l)

    heads = w_o.shape[1] // V_HEAD
    rank_kv = w_ukv.shape[0]
    tile = _tile(seq // 2, ATTN_TK)
    freqs = ROPE_THETA ** (-jnp.arange(0, QK_ROPE, 2, dtype=F32) / QK_ROPE)
    ang = positions.reshape(t).astype(F32)[:, None] * freqs
    cos, sin = jnp.cos(ang), jnp.sin(ang)
    cs = jnp.concatenate([cos, cos, -sin, sin], axis=-1)
    wd_kv = jnp.concatenate([w_dkv, _swap_halves(w_dkv[:, rank_kv:])], axis=1).astype(BF16)
    w_kv = w_ukv.reshape(rank_kv, heads, QK_NOPE + V_HEAD)
    w_k = w_kv[..., :QK_NOPE].reshape(rank_kv, heads * QK_NOPE).astype(BF16)
    w_vt = w_kv[..., QK_NOPE:].reshape(rank_kv, heads * V_HEAD).T.astype(BF16)
    kn, kr, vt = _kv_proj(xs, kv_norm_g, wd_kv, ckv_norm_g, w_k, w_vt, cs, tile)
    for jl in range(n_b):
        l = n_a + jl
        wq = w_uq[jl].reshape(-1, heads, QK_NOPE + QK_ROPE)
        wq = jnp.concatenate([wq, _swap_halves(wq[..., QK_NOPE:])], axis=-1)
        wqt = wq.reshape(-1, heads * 2 * LANES).T.astype(BF16)
        qt = _q_proj(xs, norm_g[l, 0], w_dq[jl].astype(BF16), cq_norm_g[jl], wqt, cs.T, heads, tile, 2 * tile)
        a = _attention(qt, kn, kr, vt, nb, seq, heads)
        xs = _oproj_residual(a, xs, w_o[jl].astype(BF16), norm_g[l, 1])
        xs = ffn(xs, l)
    return xs.reshape(nb, seq, d).astype(x.dtype)
```

```python
import functools
import math

import jax
import numpy as np
import jax.numpy as jnp
from jax import lax
from jax.experimental import pallas as pl
from jax.experimental.pallas import tpu as pltpu

F32 = jnp.float32
BF16 = jnp.bfloat16

EPS = 1e-6
CHUNK = 64
QK_NOPE = 128
QK_ROPE = 64
V_HEAD = 128
ROPE_THETA = 10000.0
ATTN_SCALE = (QK_NOPE + QK_ROPE) ** -0.5
SSM_L = 16
S5_GROUP_BLOCK = 8
S5_ROW_PAD = 8
LANES = 128
NEG_INF = -1e30
V7X_VMEM_BYTES = 64 * 1024 * 1024


def _cparams(semantics, vmem_mib):
    assert vmem_mib * 1024 * 1024 < V7X_VMEM_BYTES
    return pltpu.CompilerParams(dimension_semantics=semantics,
                                vmem_limit_bytes=vmem_mib * 1024 * 1024)


def _tile(n, pref):
    t = min(n, pref)
    assert n % t == 0, (n, pref)
    return t


def _rms(x, g):
    return x * lax.rsqrt(jnp.mean(x * x, axis=-1, keepdims=True) + EPS) * g


def _gelu(x):
    c = math.sqrt(2.0 / math.pi)
    return 0.5 * x * (1.0 + jnp.tanh(c * (x + 0.044715 * (x * x * x))))


def _dot(a, b):
    return jnp.dot(a, b, preferred_element_type=F32)


def _resident(shape):
    nd = len(shape)
    return pl.BlockSpec(shape, lambda *_: (0,) * nd, pipeline_mode=pl.Buffered(1))


def _norm_kernel(x_ref, g_ref, o_ref):
    o_ref[...] = _rms(x_ref[...], g_ref[...]).astype(o_ref.dtype)


def _norm_cast(x, g, dtype):
    t, d = x.shape
    tm = _tile(t, 1024)
    return pl.pallas_call(
        _norm_kernel,
        grid=(t // tm,),
        in_specs=[pl.BlockSpec((tm, d), lambda i: (i, 0)), _resident((1, d))],
        out_specs=pl.BlockSpec((tm, d), lambda i: (i, 0)),
        out_shape=jax.ShapeDtypeStruct((t, d), dtype),
        compiler_params=_cparams(("parallel",), 40),
        name="norm_cast",
    )(x, g.reshape(1, d))


def _s5_position_order(gb):
    t = np.arange(SSM_L)
    gl = np.arange(gb)[:, None]
    return (t // gb) * gb + (t % gb - gl) % gb


def _s5_weights(a_re, a_im, log_dt, b_re, b_im, c_re, c_im, d_skip, gb):
    hi = lax.Precision.HIGHEST
    g, n = a_re.shape
    p = b_re.shape[-1]
    el = SSM_L
    x = g // gb
    dt = jnp.exp(log_dt)[:, None]
    mag = jnp.exp(dt * a_re)
    ab_re = mag * jnp.cos(dt * a_im)
    ab_im = mag * jnp.sin(dt * a_im)
    den = a_re * a_re + a_im * a_im
    nr, ni = ab_re - 1.0, ab_im
    coef_re = ((nr * a_re + ni * a_im) / den)[..., None]
    coef_im = ((ni * a_re - nr * a_im) / den)[..., None]
    bb_re = coef_re * b_re - coef_im * b_im
    bb_im = coef_re * b_im + coef_im * b_re
    prs, pis = [jnp.ones_like(ab_re)], [jnp.zeros_like(ab_re)]
    for _ in range(el):
        r, i = prs[-1], pis[-1]
        prs.append(r * ab_re - i * ab_im)
        pis.append(r * ab_im + i * ab_re)
    pr, pi = jnp.stack(prs), jnp.stack(pis)
    order = _s5_position_order(gb)
    d_all = np.arange(el + 1)
    lag_sel = (order[:, None, None, :] - order[:, None, :, None] == d_all[None, :el, None, None])
    lag_sel = jnp.asarray(lag_sel, F32)
    rev_sel = jnp.asarray(el - 1 - order[:, :, None] == d_all[None, None, :], F32)
    fwd_sel = jnp.asarray(order[:, :, None] + 1 == d_all[None, None, :], F32)
    mr = c_re[None] * pr[:el, :, None, :] - c_im[None] * pi[:el, :, None, :]
    mi = c_re[None] * pi[:el, :, None, :] + c_im[None] * pr[:el, :, None, :]
    cb = (jnp.einsum("dgpn,gnq->gdpq", mr, bb_re, precision=hi)
          - jnp.einsum("dgpn,gnq->gdpq", mi, bb_im, precision=hi))
    cb = cb + (d_all[None, :el, None, None] == 0) * jnp.eye(p, dtype=F32) * d_skip[:, None, :, None]
    wy = jnp.einsum("xgdpq,gdut->xguqtp", cb.astype(BF16).reshape(x, gb, el, p, p),
                    lag_sel.astype(BF16), preferred_element_type=BF16)
    wy = wy.reshape(g, el * p, el * p)
    pr5, pi5 = pr.reshape(el + 1, x, gb, n), pi.reshape(el + 1, x, gb, n)
    prr = jnp.einsum("gtd,dxgn->xgtn", rev_sel, pr5, precision=hi)[:, :, :, None, :]
    pir = jnp.einsum("gtd,dxgn->xgtn", rev_sel, pi5, precision=hi)[:, :, :, None, :]
    bq_re = bb_re.transpose(0, 2, 1).reshape(x, gb, 1, p, n)
    bq_im = bb_im.transpose(0, 2, 1).reshape(x, gb, 1, p, n)
    q_re = prr * bq_re - pir * bq_im
    q_im = prr * bq_im + pir * bq_re
    wz = jnp.concatenate([q_re, q_im, q_im, q_re], axis=-1).reshape(g, el * p, 4 * n)
    prf = jnp.einsum("gtd,dxgn->xgnt", fwd_sel, pr5, precision=hi)[..., None]
    pif = jnp.einsum("gtd,dxgn->xgnt", fwd_sel, pi5, precision=hi)[..., None]
    cn_re = c_re.transpose(0, 2, 1).reshape(x, gb, n, 1, p)
    cn_im = c_im.transpose(0, 2, 1).reshape(x, gb, n, 1, p)
    r_re = cn_re * prf - cn_im * pif
    r_im = -(cn_re * pif + cn_im * prf)
    wr = jnp.concatenate([r_re, r_im], axis=2).reshape(g, 2 * n, el * p)
    al_re, al_im = pr[el], pi[el]
    ac = jnp.stack([jnp.concatenate([al_re, al_re], -1),
                    jnp.concatenate([-al_im, al_im], -1),
                    jnp.concatenate([al_im, -al_im], -1)])
    return wz.astype(BF16), wy.astype(BF16), wr.astype(BF16), ac


S5_ROW_SUB = 64


def _s5_kernel(u_ref, wz_ref, wy_ref, wr_ref, ac_ref, o_ref, x_scr, y_scr, z_scr, zs_scr, h_scr, *,
               gb, rt, ns):
    stride = rt + S5_ROW_PAD
    p = LANES // gb
    blk = lax.broadcasted_iota(jnp.int32, (S5_ROW_SUB, LANES), 1) // p
    for r0 in range(0, rt, S5_ROW_SUB):
        xr = []
        for s in range(SSM_L):
            x = u_ref[pl.ds(r0 * SSM_L + s, S5_ROW_SUB, stride=SSM_L), :]
            xr.append(pltpu.roll(x, (s % gb) * p, axis=1) if s % gb else x)
        for gi in range(gb):
            halves = []
            for v in range(SSM_L // gb):
                out = xr[v * gb]
                for s in range(v * gb + 1, (v + 1) * gb):
                    out = jnp.where(blk == (gi + s) % gb, xr[s], out)
                halves.append(out)
            x_scr[gi, r0:r0 + S5_ROW_SUB, :] = jnp.concatenate(halves, axis=1).astype(BF16)
    for gi in range(gb):
        z = _dot(x_scr[gi], wz_ref[gi])
        z_scr[gi * stride:gi * stride + rt, :] = z[:, :ns]
        zs_scr[gi * stride:gi * stride + rt, :] = z[:, ns:]
    ar, ai, ais = ac_ref[0], ac_ref[1], ac_ref[2]

    @pl.when(pl.program_id(2) == 0)
    def _():
        h_scr[...] = jnp.zeros_like(h_scr)

    def body(c, carry):
        h, hs = carry
        rows = pl.ds(c, gb, stride=stride)
        zc = z_scr[rows, :]
        zsc = zs_scr[rows, :]
        z_scr[rows, :] = h
        return zc + ar * h + ai * hs, zsc + ar * hs + ais * h

    h, hs = lax.fori_loop(0, rt, body, (h_scr[0], h_scr[1]), unroll=8)
    h_scr[0] = h
    h_scr[1] = hs
    for gi in range(gb):
        hin = z_scr[gi * stride:gi * stride + rt, :].astype(BF16)
        y_scr[gi] = _gelu(_dot(x_scr[gi], wy_ref[gi]) + _dot(hin, wr_ref[gi]))
    for r0 in range(0, rt, S5_ROW_SUB):
        ys = [y_scr[gi, r0:r0 + S5_ROW_SUB, :] for gi in range(gb)]
        for s in range(SSM_L):
            v = s // gb
            row = ys[0][:, v * LANES:(v + 1) * LANES]
            for gi in range(1, gb):
                row = jnp.where(blk == (gi + s) % gb, ys[gi][:, v * LANES:(v + 1) * LANES], row)
            if s % gb:
                row = pltpu.roll(row, LANES - (s % gb) * p, axis=1)
            o_ref[pl.ds(r0 * SSM_L + s, S5_ROW_SUB, stride=SSM_L), :] = row


def _s5_scan(u, wz, wy, wr, ac, nb):
    t, d = u.shape
    g, lp, _ = wy.shape
    ns = wr.shape[1]
    gb = S5_GROUP_BLOCK
    assert gb * (lp // SSM_L) == LANES and g * (lp // SSM_L) == d
    nc = t // nb // SSM_L
    rt = _tile(nc, 256)
    assert rt % S5_ROW_SUB == 0
    nh = nc // rt
    kern = functools.partial(_s5_kernel, gb=gb, rt=rt, ns=ns)
    return pl.pallas_call(
        kern,
        grid=(g // gb, nb, nh),
        in_specs=[pl.BlockSpec((rt * SSM_L, LANES), lambda i, b, r: (b * nh + r, i)),
                  pl.BlockSpec((gb, lp, 2 * ns), lambda i, b, r: (i, 0, 0)),
                  pl.BlockSpec((gb, lp, lp), lambda i, b, r: (i, 0, 0)),
                  pl.BlockSpec((gb, ns, lp), lambda i, b, r: (i, 0, 0)),
                  pl.BlockSpec((3, gb, ns), lambda i, b, r: (0, i, 0))],
        out_specs=pl.BlockSpec((rt * SSM_L, LANES), lambda i, b, r: (b * nh + r, i)),
        out_shape=jax.ShapeDtypeStruct((t, d), F32),
        scratch_shapes=[pltpu.VMEM((gb, rt, lp), BF16),
                        pltpu.VMEM((gb, rt, lp), F32),
                        pltpu.VMEM((gb * (rt + S5_ROW_PAD), ns), F32),
                        pltpu.VMEM((gb * (rt + S5_ROW_PAD), ns), F32),
                        pltpu.VMEM((2, gb, ns), F32)],
        compiler_params=_cparams(("parallel", "arbitrary", "arbitrary"), 48),
        name="s5_scan",
    )(u, wz, wy, wr, ac)


def _glu_kernel(y_ref, x_ref, w_ref, b_ref, g_ref, o_ref):
    d = x_ref.shape[1]
    y = y_ref[...].astype(BF16)
    val = _dot(y, w_ref[:, :d]) + b_ref[:, :d]
    gate = _dot(y, w_ref[:, d:]) + b_ref[:, d:]
    m = val * jax.nn.sigmoid(gate)
    o_ref[...] = x_ref[...] + _rms(m, g_ref[...])


def _glu_residual(y, x, w, b, g):
    t, d = x.shape
    tm = _tile(t, 256)
    return pl.pallas_call(
        _glu_kernel,
        grid=(t // tm,),
        in_specs=[pl.BlockSpec((tm, d), lambda i: (i, 0)),
                  pl.BlockSpec((tm, d), lambda i: (i, 0)),
                  _resident((d, 2 * d)), _resident((1, 2 * d)), _resident((1, d))],
        out_specs=pl.BlockSpec((tm, d), lambda i: (i, 0)),
        out_shape=jax.ShapeDtypeStruct((t, d), F32),
        compiler_params=_cparams(("parallel",), 48),
        name="glu_residual",
    )(y, x, w, b.reshape(1, 2 * d), g.reshape(1, d))


FFN_HALO = 8


def _ffn_kernel(x_ref, xp_ref, g2_ref, wg_ref, wu_ref, cw_ref, cb_ref, wd_ref, g3_ref, o_ref,
                h_scr, acc_scr, *, tm, tiles_per_seq):
    i = pl.program_id(0)
    j = pl.program_id(1)

    @pl.when(j == 0)
    def _():
        h_scr[FFN_HALO:, :] = _rms(x_ref[...], g2_ref[...]).astype(BF16)
        keep = jnp.where(i % tiles_per_seq != 0, 1.0, 0.0)
        h_scr[:FFN_HALO, :] = (_rms(xp_ref[...], g2_ref[...]) * keep).astype(BF16)
        acc_scr[...] = jnp.zeros_like(acc_scr)

    ge = _dot(h_scr[...], wg_ref[...])
    up = _dot(h_scr[FFN_HALO:, :], wu_ref[...])
    cw = cw_ref[...]
    conv = (cw[2:3, :] * ge[FFN_HALO:, :] + cw[1:2, :] * ge[FFN_HALO - 1:FFN_HALO - 1 + tm, :]
            + cw[0:1, :] * ge[FFN_HALO - 2:FFN_HALO - 2 + tm, :] + cb_ref[...])
    act = (_gelu(conv) * up).astype(BF16)
    acc_scr[...] += _dot(act, wd_ref[...])

    @pl.when(j == pl.num_programs(1) - 1)
    def _():
        o_ref[...] = x_ref[...] + _rms(acc_scr[...], g3_ref[...])


def _ffn_residual(x, g2, wg, wu, cw, cb, wd, g3, seq):
    t, d = x.shape
    f = wg.shape[1]
    tm = _tile(seq, 512)
    tf = _tile(f, 512)
    hb = tm // FFN_HALO
    kern = functools.partial(_ffn_kernel, tm=tm, tiles_per_seq=seq // tm)
    return pl.pallas_call(
        kern,
        grid=(t // tm, f // tf),
        in_specs=[pl.BlockSpec((tm, d), lambda i, j: (i, 0)),
                  pl.BlockSpec((FFN_HALO, d), lambda i, j: (jnp.maximum(i * hb - 1, 0), 0)),
                  _resident((1, d)),
                  pl.BlockSpec((d, tf), lambda i, j: (0, j)),
                  pl.BlockSpec((d, tf), lambda i, j: (0, j)),
                  pl.BlockSpec((3, tf), lambda i, j: (0, j)),
                  pl.BlockSpec((1, tf), lambda i, j: (0, j)),
                  pl.BlockSpec((tf, d), lambda i, j: (j, 0)),
                  _resident((1, d))],
        out_specs=pl.BlockSpec((tm, d), lambda i, j: (i, 0)),
        out_shape=jax.ShapeDtypeStruct((t, d), F32),
        scratch_shapes=[pltpu.VMEM((FFN_HALO + tm, d), BF16), pltpu.VMEM((tm, d), F32)],
        compiler_params=_cparams(("parallel", "arbitrary"), 56),
        name="ffn_residual",
    )(x, x, g2.reshape(1, d), wg, wu, cw, cb.reshape(1, f), wd, g3.reshape(1, d))


ATTN_TK = 512
Q_SCALE = ATTN_SCALE * math.log2(math.e)
_NT = (((1,), (1,)), ((), ()))


def _rope_pair(v, cs):
    p = v * cs
    return p + pltpu.roll(p, QK_ROPE, axis=1)


def _kv_kernel(x_ref, g_ref, wd_ref, cg_ref, wk_ref, wvt_ref, cs_ref, kn_ref, kr_ref, vt_ref, *, rank):
    h = _rms(x_ref[...], g_ref[...]).astype(BF16)
    ckr = _dot(h, wd_ref[...])
    c = _rms(ckr[:, :rank], cg_ref[...]).astype(BF16)
    kn_ref[...] = _dot(c, wk_ref[...]).astype(kn_ref.dtype)
    vt_ref[0] = lax.dot_general(wvt_ref[...], c, _NT, preferred_element_type=F32).astype(vt_ref.dtype)
    r = _rope_pair(ckr[:, rank:], cs_ref[...])
    lane = lax.broadcasted_iota(jnp.int32, r.shape, 1)
    kr_ref[...] = jnp.where(lane < QK_ROPE, r, 0.0).astype(kr_ref.dtype)


def _kv_proj(x, g, wd, cg, wk, wvt, cs, tm):
    t, d = x.shape
    rank = wk.shape[0]
    nk = wk.shape[1]
    nv = wvt.shape[0]
    kern = functools.partial(_kv_kernel, rank=rank)
    return pl.pallas_call(
        kern,
        grid=(t // tm,),
        in_specs=[pl.BlockSpec((tm, d), lambda i: (i, 0)),
                  _resident((1, d)), _resident((d, rank + LANES)), _resident((1, rank)),
                  _resident((rank, nk)), _resident((nv, rank)),
                  pl.BlockSpec((tm, LANES), lambda i: (i, 0))],
        out_specs=[pl.BlockSpec((tm, nk), lambda i: (i, 0)),
                   pl.BlockSpec((tm, LANES), lambda i: (i, 0)),
                   pl.BlockSpec((1, nv, tm), lambda i: (i, 0, 0))],
        out_shape=[jax.ShapeDtypeStruct((t, nk), BF16), jax.ShapeDtypeStruct((t, LANES), BF16),
                   jax.ShapeDtypeStruct((t // tm, nv, tm), BF16)],
        compiler_params=_cparams(("parallel",), 48),
        name="kv_proj",
    )(x, g.reshape(1, d), wd, cg.reshape(1, rank), wk, wvt, cs)


def _q_kernel(x_ref, g_ref, wd_ref, cg_ref, wut_ref, cst_ref, q_ref, *, heads):
    h = _rms(x_ref[...], g_ref[...]).astype(BF16)
    cq = _rms(_dot(h, wd_ref[...]), cg_ref[...]).astype(BF16)
    cst = cst_ref[...]
    w = 2 * LANES
    for hd in range(heads):
        qh = lax.dot_general(wut_ref[hd * w:(hd + 1) * w, :], cq, _NT, preferred_element_type=F32)
        q_ref[0, hd * w:hd * w + LANES, :] = (qh[:LANES] * Q_SCALE).astype(q_ref.dtype)
        p = qh[LANES:] * cst
        r = ((p[:QK_ROPE] + p[QK_ROPE:]) * Q_SCALE).astype(q_ref.dtype)
        q_ref[0, hd * w + LANES:(hd + 1) * w, :] = jnp.concatenate([r, r], axis=0)


def _q_proj(x, g, wd, cg, wut, cst, heads, tm, tq):
    t, d = x.shape
    rank = wd.shape[1]
    nq = wut.shape[0]
    per = tq // tm
    kern = functools.partial(_q_kernel, heads=heads)
    return pl.pallas_call(
        kern,
        grid=(t // tm,),
        in_specs=[pl.BlockSpec((tm, d), lambda i: (i, 0)),
                  _resident((1, d)), _resident((d, rank)), _resident((1, rank)),
                  _resident((nq, rank)),
                  pl.BlockSpec((LANES, tm), lambda i: (0, i))],
        out_specs=pl.BlockSpec((1, nq, tm), lambda i: (i // per, 0, i % per)),
        out_shape=jax.ShapeDtypeStruct((t // tq, nq, tq), BF16),
        compiler_params=_cparams(("parallel",), 48),
        name="q_proj",
    )(x, g.reshape(1, d), wd, cg.reshape(1, rank), wut, cst)


def _chunk_mask(shape):
    kc = lax.broadcasted_iota(jnp.int32, shape, 0) // CHUNK
    qc = lax.broadcasted_iota(jnp.int32, shape, 1) // CHUNK
    return kc <= qc


SUM_ROWS = 16


def _attn_kernel(qt_ref, kn_ref, kr_ref, vt_ref, o_ref, sa_scr, sb_scr, m_scr, acc_scr, *, tq, tk):
    i = pl.program_id(2)
    m_scr[...] = jnp.full_like(m_scr, NEG_INF)
    acc_scr[...] = jnp.zeros_like(acc_scr)
    ones = jnp.ones((SUM_ROWS, tk), BF16)

    def keys(j):
        rows = pl.ds(pl.multiple_of(j * tk, tk), tk)
        return jnp.concatenate([kn_ref[rows, :], kr_ref[rows, :]], axis=1)

    def scores(j, dst):
        s = _dot(keys(j), qt_ref[0])
        dst[...] = s
        return jnp.max(s, axis=0, keepdims=True)

    def consume(s, mcol, j, lanes):
        m_prev = m_scr[:, lanes]
        m_new = jnp.maximum(m_prev, mcol)
        alpha = jnp.exp2(m_prev - m_new)
        p = jnp.exp2(s - m_new).astype(BF16)
        v1 = jnp.concatenate([vt_ref[j], ones], axis=0)
        acc_scr[:, lanes] = alpha * acc_scr[:, lanes] + _dot(v1, p)
        m_scr[:, lanes] = m_new

    every = slice(0, tq)

    def pair(pp, mc_a):
        j = 2 * pp
        mc_b = scores(j + 1, sb_scr)
        consume(sa_scr[...], mc_a, j, every)
        mc_next = scores(j + 2, sa_scr)
        consume(sb_scr[...], mc_b, j + 1, every)
        return mc_next

    lax.fori_loop(0, i, pair, scores(0, sa_scr))
    s2 = _dot(keys(2 * i + 1), qt_ref[0, :, tk:])
    s = jnp.where(_chunk_mask((tk, tq)), sa_scr[...], NEG_INF)
    consume(s, jnp.max(s, axis=0, keepdims=True), 2 * i, every)
    s2 = jnp.where(_chunk_mask((tk, tk)), s2, NEG_INF)
    consume(s2, jnp.max(s2, axis=0, keepdims=True), 2 * i + 1, slice(tk, tq))
    acc = acc_scr[...]
    o_ref[...] = (acc[:V_HEAD] / acc[V_HEAD:V_HEAD + 1]).T.astype(o_ref.dtype)


def _attention(qt, kn, kr, vt, nb, seq, heads):
    tq = qt.shape[2]
    tk = vt.shape[2]
    assert tq == 2 * tk and tk % CHUNK == 0 and seq % tq == 0
    nq = seq // tq
    t = nb * seq
    kern = functools.partial(_attn_kernel, tq=tq, tk=tk)
    return pl.pallas_call(
        kern,
        grid=(nb, heads, nq),
        in_specs=[pl.BlockSpec((1, 2 * LANES, tq), lambda b, h, i: (b * nq + i, h, 0)),
                  pl.BlockSpec((seq, QK_NOPE), lambda b, h, i: (b, h)),
                  pl.BlockSpec((seq, LANES), lambda b, h, i: (b, 0)),
                  pl.BlockSpec((seq // tk, V_HEAD, tk), lambda b, h, i: (b, h, 0))],
        out_specs=pl.BlockSpec((tq, V_HEAD), lambda b, h, i: (b * nq + i, h)),
        out_shape=jax.ShapeDtypeStruct((t, heads * V_HEAD), BF16),
        scratch_shapes=[pltpu.VMEM((tk, tq), F32), pltpu.VMEM((tk, tq), F32),
                        pltpu.VMEM((1, tq), F32),
                        pltpu.VMEM((V_HEAD + SUM_ROWS, tq), F32)],
        compiler_params=_cparams(("parallel", "parallel", "arbitrary"), 48),
        name="attention",
    )(qt, kn, kr, vt)


def _oproj_kernel(a_ref, x_ref, w_ref, g_ref, o_ref):
    o_ref[...] = x_ref[...] + _rms(_dot(a_ref[...], w_ref[...]), g_ref[...])


def _oproj_residual(a, x, w, g):
    t, d = x.shape
    k = a.shape[1]
    tm = _tile(t, 512)
    return pl.pallas_call(
        _oproj_kernel,
        grid=(t // tm,),
        in_specs=[pl.BlockSpec((tm, k), lambda i: (i, 0)),
                  pl.BlockSpec((tm, d), lambda i: (i, 0)),
                  _resident((k, d)), _resident((1, d))],
        out_specs=pl.BlockSpec((tm, d), lambda i: (i, 0)),
        out_shape=jax.ShapeDtypeStruct((t, d), F32),
        compiler_params=_cparams(("parallel",), 48),
        name="oproj_residual",
    )(a, x, w, g.reshape(1, d))


def _swap_halves(w):
    half = w.shape[-1] // 2
    return jnp.concatenate([w[..., half:], w[..., :half]], axis=-1)


def kernel(x, positions, norm_g, ssm_a_re, ssm_a_im, ssm_log_dt, ssm_b_re, ssm_b_im, ssm_c_re, ssm_c_im, ssm_d, ssm_w_glu, ssm_b_glu, kv_norm_g, w_dkv, ckv_norm_g, w_ukv, w_dq, cq_norm_g, w_uq, w_o, ffn_w_gate, ffn_w_up, ffn_conv_w, ffn_conv_b, ffn_w_down):
    nb, seq, d = x.shape
    t = nb * seq
    n_a = ssm_a_re.shape[0]
    n_b = w_dq.shape[0]
    xs = x.reshape(t, d).astype(F32)

    def ffn(xs, l):
        return _ffn_residual(xs, norm_g[l, 2], ffn_w_gate[l].astype(BF16), ffn_w_up[l].astype(BF16),
                             ffn_conv_w[l], ffn_conv_b[l], ffn_w_down[l].astype(BF16), norm_g[l, 3], seq)

    for l in range(n_a):
        wz, wy, wr, ac = _s5_weights(ssm_a_re[l], ssm_a_im[l], ssm_log_dt[l], ssm_b_re[l], ssm_b_im[l],
                                     ssm_c_re[l], ssm_c_im[l], ssm_d[l], S5_GROUP_BLOCK)
        u = _norm_cast(xs, norm_g[l, 0], F32)
        y = _s5_scan(u, wz, wy, wr, ac, nb)
        xs = _glu_residual(y, xs, ssm_w_glu[l].astype(BF16), ssm_b_glu[l], norm_g[l, 1])
        xs = ffn(xs, l)

    heads = w_o.shape[1] // V_HEAD
    rank_kv = w_ukv.shape[0]
    tile = _tile(seq // 2, ATTN_TK)
    freqs = ROPE_THETA ** (-jnp.arange(0, QK_ROPE, 2, dtype=F32) / QK_ROPE)
    ang = positions.reshape(t).astype(F32)[:, None] * freqs
    cos, sin = jnp.cos(ang), jnp.sin(ang)
    cs = jnp.concatenate([cos, cos, -sin, sin], axis=-1)
    wd_kv = jnp.concatenate([w_dkv, _swap_halves(w_dkv[:, rank_kv:])], axis=1).astype(BF16)
    w_kv = w_ukv.reshape(rank_kv, heads, QK_NOPE + V_HEAD)
    w_k = w_kv[..., :QK_NOPE].reshape(rank_kv, heads * QK_NOPE).astype(BF16)
    w_vt = w_kv[..., QK_NOPE:].reshape(rank_kv, heads * V_HEAD).T.astype(BF16)
    kn, kr, vt = _kv_proj(xs, kv_norm_g, wd_kv, ckv_norm_g, w_k, w_vt, cs, tile)
    for jl in range(n_b):
        l = n_a + jl
        wq = w_uq[jl].reshape(-1, heads, QK_NOPE + QK_ROPE)
        wq = jnp.concatenate([wq, _swap_halves(wq[..., QK_NOPE:])], axis=-1)
        wqt = wq.reshape(-1, heads * 2 * LANES).T.astype(BF16)
        qt = _q_proj(xs, norm_g[l, 0], w_dq[jl].astype(BF16), cq_norm_g[jl], wqt, cs.T, heads, tile, 2 * tile)
        a = _attention(qt, kn, kr, vt, nb, seq, heads)
        xs = _oproj_residual(a, xs, w_o[jl].astype(BF16), norm_g[l, 1])
        xs = ffn(xs, l)
    return xs.reshape(nb, seq, d).astype(x.dtype)
```
